```python
import math
import jax, jax.numpy as jnp
from jax import lax
import numpy as np

D_MODEL = 1024
BATCH = 4
SEQ = 4096
DEPTH = 2

MEM_LEN = 256
D_MIX = D_MODEL
SSD_DIM = D_MIX // 2
SSD_HEAD_DIM = 64
SSD_HEADS = SSD_DIM // SSD_HEAD_DIM
SSD_GROUPS = 2
SSD_STATE = 64
CONV_K = 4
CHUNK = 128
SB_DIM = D_MIX - SSD_DIM
SB_HEAD_DIM = 64
SB_HEADS = SB_DIM // SB_HEAD_DIM
Q_BLOCK = 128
XA_HEADS = 4
XA_HEAD_DIM = 128
XA_DIM = XA_HEADS * XA_HEAD_DIM
D_FF = 4 * D_MODEL
EPS = 1e-5
GN = SSD_GROUPS * SSD_STATE
CONV_DIM = SSD_DIM + 2 * GN
IN_DIM = SSD_DIM + CONV_DIM + SSD_HEADS + 3 * SB_DIM

kernel_name = "hymba_ssd_stickbreaking_memxattn_trunk"


def rmsnorm(x, g):
    xf = x.astype(jnp.float32)
    y = xf * lax.rsqrt(jnp.mean(xf * xf, axis=-1, keepdims=True) + EPS)
    return (y * g.astype(jnp.float32)).astype(x.dtype)


def causal_dwconv(x, w, b):
    y = lax.conv_general_dilated(
        x, w.astype(x.dtype)[:, None, :], window_strides=(1,), padding=[(CONV_K - 1, 0)],
        dimension_numbers=("NWC", "WIO", "NWC"), feature_group_count=x.shape[-1])
    return y + b.astype(x.dtype)


def ssd_chunked(xh, dt, a, bm, cm):
    b, s, h, p = xh.shape
    g, n = bm.shape[-2:]
    e = h // g
    nc = s // CHUNK
    x_c = (xh * dt[..., None]).reshape(b, nc, CHUNK, g, e, p)
    a_c = (dt * a).reshape(b, nc, CHUNK, g, e)
    b_c = bm.reshape(b, nc, CHUNK, g, n)
    c_c = cm.reshape(b, nc, CHUNK, g, n)
    a_cum = jnp.cumsum(a_c, axis=2)
    seg = a_cum[:, :, :, None] - a_cum[:, :, None, :]
    causal = jnp.tril(jnp.ones((CHUNK, CHUNK), dtype=bool))[:, :, None, None]
    decay = jnp.exp(jnp.where(causal, seg, -jnp.inf))
    cb = jnp.einsum("bclgn,bcsgn->bclsg", c_c, b_c)
    y_diag = jnp.einsum("bclsg,bclsge,bcsgep->bclgep", cb, decay, x_c)
    decay_states = jnp.exp(a_cum[:, :, -1:] - a_cum)
    states = jnp.einsum("bclgn,bclge,bclgep->bcgepn", b_c, decay_states, x_c)
    chunk_decay = jnp.exp(a_cum[:, :, -1])

    def step(prev, inp):
        st, dec = inp
        return prev * dec[..., None, None] + st, prev

    init = jnp.zeros((b, g, e, p, n), jnp.float32)
    _, prev_states = lax.scan(step, init, (jnp.swapaxes(states, 0, 1), jnp.swapaxes(chunk_decay, 0, 1)))
    prev_states = jnp.swapaxes(prev_states, 0, 1)
    y_off = jnp.einsum("bclgn,bcgepn,bclge->bclgep", c_c, prev_states, jnp.exp(a_cum))
    return (y_diag + y_off).reshape(b, s, h, p)


def stick_breaking_attention(q, k, v):
    s_len, dh = q.shape[2], q.shape[3]
    scale = 1.0 / math.sqrt(dh)
    outs = []
    for i in range(s_len // Q_BLOCK):
        start = i * Q_BLOCK
        end = start + Q_BLOCK
        qb = q[:, :, start:end].astype(jnp.float32)
        kb = k[:, :, :end].astype(jnp.float32)
        vb = v[:, :, :end].astype(jnp.float32)
        logits = jnp.einsum("bhqd,bhkd->bhqk", qb, kb) * scale
        t_pos = start + jnp.arange(Q_BLOCK)[:, None]
        s_pos = jnp.arange(end)[None, :]
        mask = s_pos < t_pos
        log_stay = jnp.where(mask, jax.nn.log_sigmoid(-logits), 0.0)
        log_beta = jax.nn.log_sigmoid(logits)
        later = lax.cumsum(log_stay, axis=3, reverse=True) - log_stay
        w = jnp.where(mask, jnp.exp(log_beta + later), 0.0)
        outs.append(jnp.einsum("bhqk,bhkd->bhqd", w, vb))
    return jnp.concatenate(outs, axis=2)


def hybrid_mixer(hn, w_in, conv_w, conv_b, dt_bias, a_log, d_skip, ssd_norm_g, sb_norm_g, w_out):
    b, s, _ = hn.shape
    proj = hn @ w_in.astype(hn.dtype)
    z, xbc, dt_raw, qkv = jnp.split(
        proj, [SSD_DIM, SSD_DIM + CONV_DIM, SSD_DIM + CONV_DIM + SSD_HEADS], axis=-1)
    xbc = jax.nn.silu(causal_dwconv(xbc, conv_w, conv_b))
    xs, bm, cm = jnp.split(xbc, [SSD_DIM, SSD_DIM + GN], axis=-1)
    dt = jax.nn.softplus(dt_raw.astype(jnp.float32) + dt_bias.astype(jnp.float32))
    a = -jnp.exp(a_log.astype(jnp.float32))
    xh = xs.astype(jnp.float32).reshape(b, s, SSD_HEADS, SSD_HEAD_DIM)
    y = ssd_chunked(xh, dt, a,
                    bm.astype(jnp.float32).reshape(b, s, SSD_GROUPS, SSD_STATE),
                    cm.astype(jnp.float32).reshape(b, s, SSD_GROUPS, SSD_STATE))
    y = y + d_skip.astype(jnp.float32)[:, None] * xh
    y = y.reshape(b, s, SSD_DIM) * jax.nn.silu(z.astype(jnp.float32))
    y_ssd = rmsnorm(y, ssd_norm_g)
    q, k, v = jnp.split(qkv, 3, axis=-1)
    to_heads = lambda t: t.reshape(b, s, SB_HEADS, SB_HEAD_DIM).transpose(0, 2, 1, 3)
    o = stick_breaking_attention(to_heads(q), to_heads(k), to_heads(v))
    y_sb = rmsnorm(o.transpose(0, 2, 1, 3).reshape(b, s, SB_DIM), sb_norm_g)
    y_all = jnp.concatenate([y_ssd, y_sb], axis=-1).astype(hn.dtype)
    return y_all @ w_out.astype(hn.dtype)


def memory_cross_attention(hn, mem, g_mem, w_q, w_k, w_v, w_o):
    b, s, _ = hn.shape
    m = rmsnorm(mem, g_mem)
    q = (hn @ w_q.astype(hn.dtype)).reshape(b, s, XA_HEADS, XA_HEAD_DIM).astype(jnp.float32)
    k = (m @ w_k.astype(m.dtype)).reshape(b, -1, XA_HEADS, XA_HEAD_DIM).astype(jnp.float32)
    v = (m @ w_v.astype(m.dtype)).reshape(b, -1, XA_HEADS, XA_HEAD_DIM).astype(jnp.float32)
    logits = jnp.einsum("bshd,bmhd->bhsm", q, k) * (1.0 / math.sqrt(XA_HEAD_DIM))
    p = jax.nn.softmax(logits, axis=-1)
    o = jnp.einsum("bhsm,bmhd->bshd", p, v).reshape(b, s, XA_DIM).astype(hn.dtype)
    return o @ w_o.astype(hn.dtype)


def sqrelu_mlp(hn, w1, w2):
    u = hn @ w1.astype(hn.dtype)
    u = jnp.square(jax.nn.relu(u))
    return u @ w2.astype(hn.dtype)


def setup_inputs(seed: int = 0) -> dict:
    key = jax.random.key(seed)
    ks = jax.random.split(key, 24)
    f32 = jnp.float32
    nrm = lambda k, shape, fan_in: jax.random.normal(k, shape, f32) * (fan_in ** -0.5)
    gain = lambda k, shape: 1.0 + 0.02 * jax.random.normal(k, shape, f32)
    u = jax.random.uniform(ks[5], (DEPTH, SSD_HEADS), f32)
    dt0 = jnp.exp(u * (math.log(0.1) - math.log(0.001)) + math.log(0.001))
    dt_bias = dt0 + jnp.log(-jnp.expm1(-dt0))
    a_log = jnp.log(jax.random.uniform(ks[6], (DEPTH, SSD_HEADS), f32, minval=1.0, maxval=16.0))
    return {
        "x": jax.random.normal(ks[0], (BATCH, SEQ, D_MODEL), f32),
        "mem": jax.random.normal(ks[1], (BATCH, MEM_LEN, D_MODEL), f32),
        "norm_mix_g": gain(ks[2], (DEPTH, D_MODEL)),
        "w_in": nrm(ks[3], (DEPTH, D_MODEL, IN_DIM), D_MODEL),
        "conv_w": 0.5 * jax.random.normal(ks[4], (DEPTH, CONV_K, CONV_DIM), f32),
        "conv_b": 0.02 * jax.random.normal(ks[7], (DEPTH, CONV_DIM), f32),
        "dt_bias": dt_bias,
        "a_log": a_log,
        "d_skip": 1.0 + 0.1 * jax.random.normal(ks[8], (DEPTH, SSD_HEADS), f32),
        "ssd_norm_g": gain(ks[9], (DEPTH, SSD_DIM)),
        "sb_norm_g": gain(ks[10], (DEPTH, SB_DIM)),
        "w_out": nrm(ks[11], (DEPTH, D_MIX, D_MODEL), D_MIX),
        "norm_xa_g": gain(ks[12], (DEPTH, D_MODEL)),
        "norm_mem_g": gain(ks[13], (DEPTH, D_MODEL)),
        "w_xq": nrm(ks[14], (DEPTH, D_MODEL, XA_DIM), D_MODEL),
        "w_xk": nrm(ks[15], (DEPTH, D_MODEL, XA_DIM), D_MODEL),
        "w_xv": nrm(ks[16], (DEPTH, D_MODEL, XA_DIM), D_MODEL),
        "w_xo": nrm(ks[17], (DEPTH, XA_DIM, D_MODEL), XA_DIM),
        "norm_ff_g": gain(ks[18], (DEPTH, D_MODEL)),
        "w_ff1": nrm(ks[19], (DEPTH, D_MODEL, D_FF), D_MODEL),
        "w_ff2": nrm(ks[20], (DEPTH, D_FF, D_MODEL), D_FF),
        "final_g": gain(ks[21], (D_MODEL,)),
    }


def reference(x, mem, norm_mix_g, w_in, conv_w, conv_b, dt_bias, a_log, d_skip, ssd_norm_g,
              sb_norm_g, w_out, norm_xa_g, norm_mem_g, w_xq, w_xk, w_xv, w_xo,
              norm_ff_g, w_ff1, w_ff2, final_g):
    h = x
    for l in range(DEPTH):
        h = h + hybrid_mixer(rmsnorm(h, norm_mix_g[l]), w_in[l], conv_w[l], conv_b[l], dt_bias[l],
                             a_log[l], d_skip[l], ssd_norm_g[l], sb_norm_g[l], w_out[l])
        h = h + memory_cross_attention(rmsnorm(h, norm_xa_g[l]), mem, norm_mem_g[l],
                                       w_xq[l], w_xk[l], w_xv[l], w_xo[l])
        h = h + sqrelu_mlp(rmsnorm(h, norm_ff_g[l]), w_ff1[l], w_ff2[l])
    return rmsnorm(h, final_g)
```

```python
import functools
import math

import jax
import jax.numpy as jnp
from jax import lax
from jax.experimental import pallas as pl
from jax.experimental.pallas import tpu as pltpu

F32 = jnp.float32
BF16 = jnp.bfloat16

EPS = 1e-5
CONV_K = 4
CHUNK = 128
SSD_HEAD_DIM = 64
SSD_GROUPS = 2
SSD_STATE = 64
SB_HEAD_DIM = 64
XA_HEAD_DIM = 128

LANES = 128
SUBLANES = 8
VMEM_LIMIT = 56 * 1024 * 1024

SB_TQ = 128
SB_TK = 128
SB_SKIP_BELOW = -110.0
NEG_BIG = -1e30


def _cparams(sem):
    return pltpu.CompilerParams(dimension_semantics=sem, vmem_limit_bytes=VMEM_LIMIT)


def _rms(x, g):
    return x * lax.rsqrt(jnp.mean(x * x, axis=-1, keepdims=True) + EPS) * g


def _softplus(x):
    return jnp.maximum(x, 0.0) + jnp.log(1.0 + jnp.exp(-jnp.abs(x)))


def _dot(a, b):
    return jnp.dot(a, b, preferred_element_type=F32)


def _dot_nt(a, b):
    return lax.dot_general(a, b, (((1,), (1,)), ((), ())), preferred_element_type=F32)


def _split_bf16(x, parts):
    out = []
    r = x
    for _ in range(parts - 1):
        p = r.astype(BF16)
        out.append(p)
        r = r - p.astype(F32)
    out.append(r.astype(BF16))
    return out


def _inproj_kernel(h_ref, g_ref, wzx_ref, wdt_ref, wqkv_ref, zx_ref, dt_ref, qkv_ref):
    hn = _rms(h_ref[...], g_ref[...]).astype(BF16)
    zx_ref[...] = _dot(hn, wzx_ref[...])
    dt_ref[...] = _dot(hn, wdt_ref[...])
    qkv_ref[...] = _dot(hn, wqkv_ref[...]).astype(BF16)


def _in_proj(h, g, wzx, wdt, wqkv, tm=512):
    t, d = h.shape
    nzx, ndt, nqkv = wzx.shape[1], wdt.shape[1], wqkv.shape[1]
    const = lambda i: (0, 0)
    row = lambda i: (i, 0)
    return pl.pallas_call(
        _inproj_kernel,
        grid=(t // tm,),
        in_specs=[
            pl.BlockSpec((tm, d), row),
            pl.BlockSpec((1, d), const),
            pl.BlockSpec((d, nzx), const),
            pl.BlockSpec((d, ndt), const),
            pl.BlockSpec((d, nqkv), const),
        ],
        out_specs=[
            pl.BlockSpec((tm, nzx), row),
            pl.BlockSpec((tm, ndt), row),
            pl.BlockSpec((tm, nqkv), row),
        ],
        out_shape=[
            jax.ShapeDtypeStruct((t, nzx), F32),
            jax.ShapeDtypeStruct((t, ndt), F32),
            jax.ShapeDtypeStruct((t, nqkv), BF16),
        ],
        compiler_params=_cparams(("parallel",)),
        name="in_proj",
    )(h, g, wzx, wdt, wqkv)


def _expand_heads(a, lane):
    cols = []
    for j in range(4):
        lo = jnp.broadcast_to(a[:, 2 * j:2 * j + 1], (a.shape[0], LANES))
        hi = jnp.broadcast_to(a[:, 2 * j + 1:2 * j + 2], (a.shape[0], LANES))
        cols.append(jnp.where(lane < SSD_HEAD_DIM, lo, hi))
    return jnp.concatenate(cols, axis=1)


def _ssd_kernel(zx_ref, dt_ref, cw_ref, cb_ref, dtb_ref, alog_ref, dsk_ref, gn_ref, y_ref, cbuf, state):
    n_inner = 512
    halo = SUBLANES
    c = pl.program_id(1)

    @pl.when(c == 0)
    def _():
        cbuf[0:halo, :] = jnp.zeros((halo, cbuf.shape[1]), F32)
        state[...] = jnp.zeros(state.shape, F32)

    xbc_raw = zx_ref[:, n_inner:]
    cbuf[halo:halo + CHUNK, :] = xbc_raw
    conv = cb_ref[...] + cw_ref[0:1, :] * cbuf[halo - 3:halo - 3 + CHUNK, :]
    for k in range(1, CONV_K):
        conv = conv + cw_ref[k:k + 1, :] * cbuf[halo - 3 + k:halo - 3 + k + CHUNK, :]
    cbuf[0:halo, :] = xbc_raw[CHUNK - halo:, :]
    xbc = conv * jax.nn.sigmoid(conv)
    xs = xbc[:, :n_inner]
    bm = xbc[:, n_inner:n_inner + LANES]
    cm = xbc[:, n_inner + LANES:]

    row = lax.broadcasted_iota(jnp.int32, (CHUNK, CHUNK), 0)
    col = lax.broadcasted_iota(jnp.int32, (CHUNK, CHUNK), 1)
    causal = row >= col
    lane = col

    dt = _softplus(dt_ref[...] + dtb_ref[...])
    a_c = dt * (-jnp.exp(alog_ref[...]))
    tri = jnp.where(causal, 1.0, 0.0).astype(BF16)
    a_cum = None
    for part in _split_bf16(a_c, 3):
        term = _dot(tri, part)
        a_cum = term if a_cum is None else a_cum + term
    a_cum_t = a_cum.T

    dt_x = _expand_heads(dt, lane)
    acum_x = _expand_heads(a_cum, lane)
    last_x = acum_x[CHUNK - 1:CHUNK, :]
    xc = xs * dt_x
    e_acum = jnp.exp(acum_x)
    xd = (xc * jnp.exp(last_x - acum_x)).astype(BF16)
    chunk_decay = jnp.exp(last_x)

    bm_b = bm.astype(BF16)
    cbs = []
    for g in range(SSD_GROUPS):
        in_g = (lane >= g * SSD_STATE) & (lane < (g + 1) * SSD_STATE)
        cbs.append(_dot_nt(jnp.where(in_g, cm, 0.0).astype(BF16), bm_b))
    y_cols = []
    for j in range(4):
        ms = []
        for h in (2 * j, 2 * j + 1):
            seg = a_cum[:, h:h + 1] - a_cum_t[h:h + 1, :]
            decay = jnp.exp(jnp.where(causal, seg, NEG_BIG))
            ms.append((cbs[h // 4] * decay).astype(BF16))
        xcj = xc[:, j * LANES:(j + 1) * LANES]
        x_lo = jnp.where(lane < SSD_HEAD_DIM, xcj, 0.0).astype(BF16)
        x_hi = jnp.where(lane >= SSD_HEAD_DIM, xcj, 0.0).astype(BF16)
        y_cols.append(_dot(jnp.concatenate(ms, axis=1), jnp.concatenate([x_lo, x_hi], axis=0)))
    y = jnp.concatenate(y_cols, axis=1)

    st = state[...]
    y = y + _dot(cm.astype(BF16), st.astype(BF16)) * e_acum
    srow = lax.broadcasted_iota(jnp.int32, st.shape, 0)
    scol = lax.broadcasted_iota(jnp.int32, st.shape, 1)
    same_group = (srow // SSD_STATE) == (scol // (n_inner // SSD_GROUPS))
    new_state = _dot(bm.T.astype(BF16), xd)
    state[...] = st * chunk_decay + jnp.where(same_group, new_state, 0.0)

    y = y + dsk_ref[...] * xs
    z = zx_ref[:, :n_inner]
    y = y * (z * jax.nn.sigmoid(z))
    y_ref[...] = _rms(y, gn_ref[...]).astype(BF16)


def _ssd(zx, dt, cw, cb, dtb, alog, dsk, gn, batch):
    t, nzx = zx.shape
    n_inner = 512
    n_conv = nzx - n_inner
    nc = t // batch // CHUNK
    const = lambda b, c: (0, 0)
    row = lambda b, c: (b * nc + c, 0)
    return pl.pallas_call(
        _ssd_kernel,
        grid=(batch, nc),
        in_specs=[
            pl.BlockSpec((CHUNK, nzx), row),
            pl.BlockSpec((CHUNK, LANES), row),
            pl.BlockSpec((CONV_K, n_conv), const),
            pl.BlockSpec((1, n_conv), const),
            pl.BlockSpec((1, LANES), const),
            pl.BlockSpec((1, LANES), const),
            pl.BlockSpec((1, n_inner), const),
            pl.BlockSpec((1, n_inner), const),
        ],
        out_specs=pl.BlockSpec((CHUNK, n_inner), row),
        out_shape=jax.ShapeDtypeStruct((t, n_inner), BF16),
        scratch_shapes=[
            pltpu.VMEM((SUBLANES + CHUNK, n_conv), F32),
            pltpu.VMEM((SSD_GROUPS * SSD_STATE, n_inner), F32),
        ],
        compiler_params=_cparams(("arbitrary", "arbitrary")),
        name="ssd",
    )(zx, dt, cw, cb, dtb, alog, dsk, gn)


def _sb_kernel(q_ref, k_ref, v_ref, u2_ref, g_ref, o_ref, acc, rsum):
    n_pairs = acc.shape[0]
    qi = pl.program_id(1)
    acc[...] = jnp.zeros(acc.shape, F32)
    rsum[...] = jnp.zeros(rsum.shape, F32)

    row = lax.broadcasted_iota(jnp.int32, (SB_TQ, SB_TK), 0)
    col = lax.broadcasted_iota(jnp.int32, (SB_TQ, SB_TK), 1)
    diff2 = jnp.concatenate([col - row] * 2, axis=1)
    lo_m = jnp.where(col < SB_HEAD_DIM, 1.0, 0.0).astype(BF16)
    hi_m = jnp.where(col >= SB_HEAD_DIM, 1.0, 0.0).astype(BF16)

    def cond(carry):
        j, rmax = carry
        return jnp.logical_and(j >= 0, rmax > SB_SKIP_BELOW)

    def body(carry):
        j, _ = carry
        off = pl.multiple_of(j * SB_TK, SB_TK)
        mask = diff2 < (qi * SB_TQ - j * SB_TK)
        rmax = None
        for p in range(n_pairs):
            cs = slice(p * LANES, (p + 1) * LANES)
            kb = k_ref[pl.ds(off, SB_TK), cs]
            vb = v_ref[pl.ds(off, SB_TK), cs]
            k2 = jnp.concatenate([kb * lo_m, kb * hi_m], axis=0)
            v2 = jnp.concatenate([vb * lo_m, vb * hi_m], axis=0)
            z = _dot_nt(q_ref[:, cs], k2)
            sp = _softplus(z)
            log_stay = jnp.where(mask, -sp, 0.0)
            log_beta = jnp.where(mask, z - sp, NEG_BIG)
            hi, lo = _split_bf16(log_stay, 2)
            cums = _dot(jnp.concatenate([hi, lo], axis=0), u2_ref[...])
            r_prev = jnp.concatenate([rsum[2 * p], rsum[2 * p + 1]], axis=1)
            later = cums[:SB_TQ] + cums[SB_TQ:] + r_prev
            w = jnp.exp(log_beta + later)
            acc[p] += _dot(w.astype(BF16), v2)
            tot = later + log_stay
            for e in range(2):
                r_new = jnp.broadcast_to(tot[:, e * SB_TK:e * SB_TK + 1], (SB_TQ, LANES))
                rsum[2 * p + e] = r_new
                rmax = r_new if rmax is None else jnp.maximum(rmax, r_new)
        return j - 1, jnp.max(rmax)

    lax.while_loop(cond, body, (qi, jnp.float32(0.0)))

    o = jnp.concatenate([acc[p] for p in range(n_pairs)], axis=1)
    o_ref[...] = _rms(o, g_ref[...]).astype(BF16)


def _sb_attention(qkv, u2, g, batch):
    t, n3 = qkv.shape
    n = n3 // 3
    s = t // batch
    nq = s // SB_TQ
    n_pairs = n // LANES
    return pl.pallas_call(
        _sb_kernel,
        grid=(batch, nq),
        in_specs=[
            pl.BlockSpec((SB_TQ, n), lambda b, i: (b * nq + i, 0)),
            pl.BlockSpec((s, n), lambda b, i: (b, 1)),
            pl.BlockSpec((s, n), lambda b, i: (b, 2)),
            pl.BlockSpec((2 * SB_TK, 2 * SB_TK), lambda b, i: (0, 0)),
            pl.BlockSpec((1, n), lambda b, i: (0, 0)),
        ],
        out_specs=pl.BlockSpec((SB_TQ, n), lambda b, i: (b * nq + i, 0)),
        out_shape=jax.ShapeDtypeStruct((t, n), BF16),
        scratch_shapes=[
            pltpu.VMEM((n_pairs, SB_TQ, LANES), F32),
            pltpu.VMEM((2 * n_pairs, SB_TQ, LANES), F32),
        ],
        compiler_params=_cparams(("parallel", "arbitrary")),
        name="sb_attn",
    )(qkv, qkv, qkv, u2, g)


def _mixout_kernel(h_ref, ya_ref, yb_ref, wa_ref, wb_ref, o_ref):
    o_ref[...] = h_ref[...] + _dot(ya_ref[...], wa_ref[...]) + _dot(yb_ref[...], wb_ref[...])


def _mix_out(h, ya, yb, wa, wb, tm=512):
    t, d = h.shape
    na, nb = ya.shape[1], yb.shape[1]
    const = lambda i: (0, 0)
    row = lambda i: (i, 0)
    return pl.pallas_call(
        _mixout_kernel,
        grid=(t // tm,),
        in_specs=[
            pl.BlockSpec((tm, d), row),
            pl.BlockSpec((tm, na), row),
            pl.BlockSpec((tm, nb), row),
            pl.BlockSpec((na, d), const),
            pl.BlockSpec((nb, d), const),
        ],
        out_specs=pl.BlockSpec((tm, d), row),
        out_shape=jax.ShapeDtypeStruct((t, d), F32),
        compiler_params=_cparams(("parallel",)),
        name="mix_out",
    )(h, ya, yb, wa, wb)


def _memkv_kernel(m_ref, g_ref, wk_ref, wv_ref, k_ref, v_ref):
    mn = _rms(m_ref[...], g_ref[...]).astype(BF16)
    k_ref[...] = _dot(mn, wk_ref[...]).astype(BF16)
    v_ref[...] = _dot(mn, wv_ref[...]).astype(BF16)


def _mem_kv(mem2d, g, wk, wv, batch):
    tmem, d = mem2d.shape
    m = tmem // batch
    n = wk.shape[1]
    const = lambda b: (0, 0)
    row = lambda b: (b, 0)
    return pl.pallas_call(
        _memkv_kernel,
        grid=(batch,),
        in_specs=[
            pl.BlockSpec((m, d), row),
            pl.BlockSpec((1, d), const),
            pl.BlockSpec((d, n), const),
            pl.BlockSpec((d, n), const),
        ],
        out_specs=[pl.BlockSpec((m, n), row), pl.BlockSpec((m, n), row)],
        out_shape=[jax.ShapeDtypeStruct((tmem, n), BF16)] * 2,
        compiler_params=_cparams(("parallel",)),
        name="mem_kv",
    )(mem2d, g, wk, wv)


def _xattn_kernel(h_ref, g_ref, wq_ref, k_ref, v_ref, wo_ref, o_ref):
    h = h_ref[...]
    hn = _rms(h, g_ref[...]).astype(BF16)
    q = _dot(hn, wq_ref[...]).astype(BF16)
    n_heads = q.shape[1] // XA_HEAD_DIM
    scale = 1.0 / math.sqrt(XA_HEAD_DIM)
    outs = []
    for hd in range(n_heads):
        cs = slice(hd * XA_HEAD_DIM, (hd + 1) * XA_HEAD_DIM)
        logits = _dot_nt(q[:, cs], k_ref[:, cs]) * scale
        e = jnp.exp(logits - jnp.max(logits, axis=-1, keepdims=True))
        denom = jnp.sum(e, axis=-1, keepdims=True)
        outs.append(_dot(e.astype(BF16), v_ref[:, cs]) / denom)
    o = jnp.concatenate(outs, axis=1).astype(BF16)
    o_ref[...] = h + _dot(o, wo_ref[...])


def _xattn(h, g, wq, k, v, wo, batch, tm=512):
    t, d = h.shape
    n = wq.shape[1]
    m = k.shape[0] // batch
    tiles_per_batch = t // batch // tm
    const = lambda i: (0, 0)
    row = lambda i: (i, 0)
    kv = lambda i: (i // tiles_per_batch, 0)
    return pl.pallas_call(
        _xattn_kernel,
        grid=(t // tm,),
        in_specs=[
            pl.BlockSpec((tm, d), row),
            pl.BlockSpec((1, d), const),
            pl.BlockSpec((d, n), const),
            pl.BlockSpec((m, n), kv),
            pl.BlockSpec((m, n), kv),
            pl.BlockSpec((n, d), const),
        ],
        out_specs=pl.BlockSpec((tm, d), row),
        out_shape=jax.ShapeDtypeStruct((t, d), F32),
        compiler_params=_cparams(("parallel",)),
        name="xattn",
    )(h, g, wq, k, v, wo)


def _mlp_kernel(h_ref, g_ref, w1_ref, w2_ref, gf_ref, o_ref, *, ff_chunk, final_norm):
    h = h_ref[...]
    hn = _rms(h, g_ref[...]).astype(BF16)
    out = h
    for c in range(w1_ref.shape[1] // ff_chunk):
        cs = slice(c * ff_chunk, (c + 1) * ff_chunk)
        u = jnp.maximum(_dot(hn, w1_ref[:, cs]), 0.0)
        out = out + _dot((u * u).astype(BF16), w2_ref[cs, :])
    if final_norm:
        out = _rms(out, gf_ref[...])
    o_ref[...] = out


def _mlp(h, g, w1, w2, gf, final_norm, tm=512, ff_chunk=1024):
    t, d = h.shape
    f = w1.shape[1]
    const = lambda i: (0, 0)
    row = lambda i: (i, 0)
    return pl.pallas_call(
        functools.partial(_mlp_kernel, ff_chunk=ff_chunk, final_norm=final_norm),
        grid=(t // tm,),
        in_specs=[
            pl.BlockSpec((tm, d), row),
            pl.BlockSpec((1, d), const),
            pl.BlockSpec((d, f), const),
            pl.BlockSpec((f, d), const),
            pl.BlockSpec((1, d), const),
        ],
        out_specs=pl.BlockSpec((tm, d), row),
        out_shape=jax.ShapeDtypeStruct((t, d), F32),
        compiler_params=_cparams(("parallel",)),
        name="mlp",
    )(h, g, w1, w2, gf)


def _pad_lanes(v):
    return jnp.pad(v.astype(F32), (0, LANES - v.shape[0])).reshape(1, LANES)


def kernel(x, mem, norm_mix_g, w_in, conv_w, conv_b, dt_bias, a_log, d_skip, ssd_norm_g, sb_norm_g, w_out,
           norm_xa_g, norm_mem_g, w_xq, w_xk, w_xv, w_xo, norm_ff_g, w_ff1, w_ff2, final_g):
    batch, seq, d = x.shape
    depth = w_in.shape[0]
    n_heads = dt_bias.shape[1]
    n_inner = n_heads * SSD_HEAD_DIM
    n_conv = n_inner + 2 * SSD_GROUPS * SSD_STATE
    n_sb = (w_in.shape[2] - n_inner - n_conv - n_heads) // 3
    t = batch * seq
    row = lambda v: v.astype(F32).reshape(1, -1)

    jj = lax.broadcasted_iota(jnp.int32, (2 * SB_TK, 2 * SB_TK), 0)
    ss = lax.broadcasted_iota(jnp.int32, (2 * SB_TK, 2 * SB_TK), 1)
    u2 = ((jj // SB_TK == ss // SB_TK) & (jj > ss)).astype(BF16)

    h = x.reshape(t, d)
    mem2d = mem.reshape(batch * mem.shape[1], d)
    for l in range(depth):
        o_dt = n_inner + n_conv
        o_q = o_dt + n_heads
        w_zx = w_in[l, :, :o_dt].astype(BF16)
        w_dt = jnp.pad(w_in[l, :, o_dt:o_q], ((0, 0), (0, LANES - n_heads))).astype(BF16)
        q_scale = 1.0 / math.sqrt(SB_HEAD_DIM)
        w_qkv = jnp.concatenate([w_in[l, :, o_q:o_q + n_sb] * q_scale, w_in[l, :, o_q + n_sb:]], axis=1).astype(BF16)

        zx, dt, qkv = _in_proj(h, row(norm_mix_g[l]), w_zx, w_dt, w_qkv)
        y_ssd = _ssd(zx, dt, conv_w[l].astype(F32), row(conv_b[l]), _pad_lanes(dt_bias[l]), _pad_lanes(a_log[l]),
                     row(jnp.repeat(d_skip[l], SSD_HEAD_DIM)), row(ssd_norm_g[l]), batch)
        y_sb = _sb_attention(qkv, u2, row(sb_norm_g[l]), batch)
        h = _mix_out(h, y_ssd, y_sb, w_out[l, :n_inner].astype(BF16), w_out[l, n_inner:].astype(BF16))

        k_mem, v_mem = _mem_kv(mem2d, row(norm_mem_g[l]), w_xk[l].astype(BF16), w_xv[l].astype(BF16), batch)
        h = _xattn(h, row(norm_xa_g[l]), w_xq[l].astype(BF16), k_mem, v_mem, w_xo[l].astype(BF16), batch)
        h = _mlp(h, row(norm_ff_g[l]), w_ff1[l].astype(BF16), w_ff2[l].astype(BF16), row(final_g),
                 final_norm=(l == depth - 1))
    return h.reshape(batch, seq, d)
```

```python
import functools
import math

import jax
import jax.numpy as jnp
from jax import lax
from jax.experimental import pallas as pl
from jax.experimental.pallas import tpu as pltpu

F32 = jnp.float32
BF16 = jnp.bfloat16

EPS = 1e-5
CONV_K = 4
CHUNK = 128
SSD_HEAD_DIM = 64
SSD_GROUPS = 2
SSD_STATE = 64
SB_HEAD_DIM = 64
XA_HEAD_DIM = 128

LANES = 128
SUBLANES = 8
VMEM_LIMIT = 56 * 1024 * 1024

SB_TQ = 128
SB_TK = 128
SB_GROUP = 2
SB_SKIP_BELOW = -110.0
NEG_BIG = -1e30
LOG2_E = math.log2(math.e)


def _cparams(sem):
    return pltpu.CompilerParams(dimension_semantics=sem, vmem_limit_bytes=VMEM_LIMIT)


def _rms(x, g):
    return x * lax.rsqrt(jnp.mean(x * x, axis=-1, keepdims=True) + EPS) * g


def _softplus(x):
    return jnp.maximum(x, 0.0) + jnp.log(1.0 + jnp.exp(-jnp.abs(x)))


def _dot(a, b):
    return jnp.dot(a, b, preferred_element_type=F32)


def _dot_nt(a, b):
    return lax.dot_general(a, b, (((1,), (1,)), ((), ())), preferred_element_type=F32)


def _split_bf16(x, parts):
    out = []
    r = x
    for _ in range(parts - 1):
        p = r.astype(BF16)
        out.append(p)
        r = r - p.astype(F32)
    out.append(r.astype(BF16))
    return out


def _inproj_kernel(h_ref, g_ref, wzx_ref, wdt_ref, wqkv_ref, zx_ref, dt_ref, qkv_ref):
    hn = _rms(h_ref[...], g_ref[...]).astype(BF16)
    zx_ref[...] = _dot(hn, wzx_ref[...])
    dt_ref[...] = _dot(hn, wdt_ref[...])
    qkv_ref[...] = _dot(hn, wqkv_ref[...]).astype(BF16)


def _in_proj(h, g, wzx, wdt, wqkv, tm=512):
    t, d = h.shape
    nzx, ndt, nqkv = wzx.shape[1], wdt.shape[1], wqkv.shape[1]
    const = lambda i: (0, 0)
    row = lambda i: (i, 0)
    return pl.pallas_call(
        _inproj_kernel,
        grid=(t // tm,),
        in_specs=[
            pl.BlockSpec((tm, d), row),
            pl.BlockSpec((1, d), const),
            pl.BlockSpec((d, nzx), const),
            pl.BlockSpec((d, ndt), const),
            pl.BlockSpec((d, nqkv), const),
        ],
        out_specs=[
            pl.BlockSpec((tm, nzx), row),
            pl.BlockSpec((tm, ndt), row),
            pl.BlockSpec((tm, nqkv), row),
        ],
        out_shape=[
            jax.ShapeDtypeStruct((t, nzx), F32),
            jax.ShapeDtypeStruct((t, ndt), F32),
            jax.ShapeDtypeStruct((t, nqkv), BF16),
        ],
        compiler_params=_cparams(("parallel",)),
        name="in_proj",
    )(h, g, wzx, wdt, wqkv)


def _expand_heads(a, lane):
    cols = []
    for j in range(4):
        lo = jnp.broadcast_to(a[:, 2 * j:2 * j + 1], (a.shape[0], LANES))
        hi = jnp.broadcast_to(a[:, 2 * j + 1:2 * j + 2], (a.shape[0], LANES))
        cols.append(jnp.where(lane < SSD_HEAD_DIM, lo, hi))
    return jnp.concatenate(cols, axis=1)


def _ssd_kernel(zx_ref, dt_ref, cw_ref, cb_ref, dtb_ref, alog_ref, dsk_ref, gn_ref, y_ref, cbuf, state):
    n_inner = y_ref.shape[1]
    halo = SUBLANES
    c = pl.program_id(1)

    @pl.when(c == 0)
    def _():
        cbuf[0:halo, :] = jnp.zeros((halo, cbuf.shape[1]), F32)
        state[...] = jnp.zeros(state.shape, F32)

    xbc_raw = zx_ref[:, n_inner:]
    cbuf[halo:halo + CHUNK, :] = xbc_raw
    window = cbuf[...]
    conv = cb_ref[...] + cw_ref[CONV_K - 1] * xbc_raw
    for k in range(CONV_K - 1):
        shifted = pltpu.roll(window, CONV_K - 1 - k, 0)[halo:, :]
        conv = conv + cw_ref[k] * shifted
    cbuf[0:halo, :] = xbc_raw[CHUNK - halo:, :]
    xbc = conv * jax.nn.sigmoid(conv)
    xs = xbc[:, :n_inner]
    bm = xbc[:, n_inner:n_inner + LANES]
    cm = xbc[:, n_inner + LANES:]

    row = lax.broadcasted_iota(jnp.int32, (CHUNK, CHUNK), 0)
    col = lax.broadcasted_iota(jnp.int32, (CHUNK, CHUNK), 1)
    causal = row >= col
    lane = col

    dt = _softplus(dt_ref[...] + dtb_ref[...])
    a_c = dt * (-jnp.exp(alog_ref[...]))
    tri = jnp.where(causal, 1.0, 0.0).astype(BF16)
    a_cum = None
    for part in _split_bf16(a_c, 3):
        term = _dot(tri, part)
        a_cum = term if a_cum is None else a_cum + term
    a_cum_t = a_cum.T

    dt_x = _expand_heads(dt, lane)
    acum_x = _expand_heads(a_cum, lane)
    last_x = acum_x[CHUNK - 1:CHUNK, :]
    xc = xs * dt_x
    e_acum = jnp.exp(acum_x)
    xd = (xc * jnp.exp(last_x - acum_x)).astype(BF16)
    chunk_decay = jnp.exp(last_x)

    bm_b = bm.astype(BF16)
    cbs = []
    for g in range(SSD_GROUPS):
        in_g = (lane >= g * SSD_STATE) & (lane < (g + 1) * SSD_STATE)
        cbs.append(_dot_nt(jnp.where(in_g, cm, 0.0).astype(BF16), bm_b))
    y_cols = []
    for j in range(4):
        ms = []
        for h in (2 * j, 2 * j + 1):
            seg = a_cum[:, h:h + 1] - a_cum_t[h:h + 1, :]
            decay = jnp.exp(jnp.where(causal, seg, NEG_BIG))
            ms.append((cbs[h // 4] * decay).astype(BF16))
        xcj = xc[:, j * LANES:(j + 1) * LANES]
        x_lo = jnp.where(lane < SSD_HEAD_DIM, xcj, 0.0).astype(BF16)
        x_hi = jnp.where(lane >= SSD_HEAD_DIM, xcj, 0.0).astype(BF16)
        y_cols.append(_dot(jnp.concatenate(ms, axis=1), jnp.concatenate([x_lo, x_hi], axis=0)))
    y = jnp.concatenate(y_cols, axis=1)

    st = state[...]
    y = y + _dot(cm.astype(BF16), st.astype(BF16)) * e_acum
    srow = lax.broadcasted_iota(jnp.int32, st.shape, 0)
    scol = lax.broadcasted_iota(jnp.int32, st.shape, 1)
    same_group = (srow // SSD_STATE) == (scol // (n_inner // SSD_GROUPS))
    new_state = _dot(bm.T.astype(BF16), xd)
    state[...] = st * chunk_decay + jnp.where(same_group, new_state, 0.0)

    y = y + dsk_ref[...] * xs
    z = zx_ref[:, :n_inner]
    y = y * (z * jax.nn.sigmoid(z))
    y_ref[...] = _rms(y, gn_ref[...]).astype(BF16)


def _ssd(zx, dt, cw, cb, dtb, alog, dsk, gn, batch):
    t, nzx = zx.shape
    n_inner = gn.shape[1]
    n_conv = nzx - n_inner
    nc = t // batch // CHUNK
    const = lambda b, c: (0, 0)
    row = lambda b, c: (b * nc + c, 0)
    return pl.pallas_call(
        _ssd_kernel,
        grid=(batch, nc),
        in_specs=[
            pl.BlockSpec((CHUNK, nzx), row),
            pl.BlockSpec((CHUNK, LANES), row),
            pl.BlockSpec((CONV_K, 1, n_conv), lambda b, c: (0, 0, 0)),
            pl.BlockSpec((1, n_conv), const),
            pl.BlockSpec((1, LANES), const),
            pl.BlockSpec((1, LANES), const),
            pl.BlockSpec((1, n_inner), const),
            pl.BlockSpec((1, n_inner), const),
        ],
        out_specs=pl.BlockSpec((CHUNK, n_inner), row),
        out_shape=jax.ShapeDtypeStruct((t, n_inner), BF16),
        scratch_shapes=[
            pltpu.VMEM((SUBLANES + CHUNK, n_conv), F32),
            pltpu.VMEM((SSD_GROUPS * SSD_STATE, n_inner), F32),
        ],
        compiler_params=_cparams(("arbitrary", "arbitrary")),
        name="ssd",
    )(zx, dt, cw, cb, dtb, alog, dsk, gn)


def _sb_kernel(q_ref, k_ref, v_ref, u2_ref, g_ref, o_ref, acc, rsum, z_scr, hl_scr, s_scr, w_scr):
    n_group = q_ref.shape[0]
    n_pairs = q_ref.shape[2] // LANES
    units = [(g, p) for g in range(n_group) for p in range(n_pairs)]
    qi = pl.program_id(1)
    acc[...] = jnp.zeros(acc.shape, F32)
    rsum[...] = jnp.zeros(rsum.shape, F32)

    row2 = lax.broadcasted_iota(jnp.int32, (SB_TQ, 2 * SB_TK), 0)
    col2 = lax.broadcasted_iota(jnp.int32, (SB_TQ, 2 * SB_TK), 1)
    below = (col2 & (SB_TK - 1)) < row2
    lane = lax.broadcasted_iota(jnp.int32, (SB_TK, LANES), 1)
    lo_m = jnp.where(lane < SB_HEAD_DIM, 1.0, 0.0).astype(BF16)
    hi_m = jnp.where(lane >= SB_HEAD_DIM, 1.0, 0.0).astype(BF16)

    def walk_block(j, diagonal):
        off = pl.multiple_of(j * SB_TK, SB_TK)
        for u, (g, p) in enumerate(units):
            cs = slice(p * LANES, (p + 1) * LANES)
            kb = k_ref[g, pl.ds(off, SB_TK), cs]
            k2 = jnp.concatenate([kb * lo_m, kb * hi_m], axis=0)
            z = _dot_nt(q_ref[g, :, cs], k2)
            z_scr[u] = z * LOG2_E
        for u in range(len(units)):
            z2 = z_scr[u]
            sp2 = jnp.maximum(z2, 0.0) + jnp.log2(1.0 + jnp.exp2(-jnp.abs(z2)))
            if diagonal:
                sp2 = jnp.where(below, sp2, 0.0)
                z_scr[u] = jnp.where(below, z2, NEG_BIG)
            hi, lo = _split_bf16(sp2, 2)
            hl_scr[u, :, 0:2 * SB_TK] = hi
            hl_scr[u, :, 2 * SB_TK:4 * SB_TK] = lo
        for u in range(len(units)):
            s_scr[u] = _dot(hl_scr[u], u2_ref[...])
        rmax = None
        for u in range(len(units)):
            r_prev = jnp.concatenate([rsum[2 * u], rsum[2 * u + 1]], axis=1)
            s = s_scr[u] + r_prev
            w_scr[u] = jnp.exp2(z_scr[u] + s).astype(BF16)
            for e in range(2):
                r_new = jnp.broadcast_to(s[:, e * SB_TK:e * SB_TK + 1], (SB_TQ, LANES))
                rsum[2 * u + e] = r_new
                rmax = r_new if rmax is None else jnp.maximum(rmax, r_new)
        for u, (g, p) in enumerate(units):
            cs = slice(p * LANES, (p + 1) * LANES)
            vb = v_ref[g, pl.ds(off, SB_TK), cs]
            v2 = jnp.concatenate([vb * lo_m, vb * hi_m], axis=0)
            acc[u] += _dot(w_scr[u], v2)
        return jnp.max(rmax)

    def cond(carry):
        j, rmax = carry
        return jnp.logical_and(j >= 0, rmax > SB_SKIP_BELOW * LOG2_E)

    def body(carry):
        j, _ = carry
        return j - 1, walk_block(j, False)

    lax.while_loop(cond, body, (qi - 1, walk_block(qi, True)))

    for g in range(n_group):
        o = jnp.concatenate([acc[g * n_pairs + p] for p in range(n_pairs)], axis=1)
        o_ref[g] = _rms(o, g_ref[...]).astype(BF16)


def _sb_attention(qkv, u2, g, batch):
    t, n3 = qkv.shape
    n = n3 // 3
    s = t // batch
    nq = s // SB_TQ
    n_pairs = n // LANES
    group = SB_GROUP if batch % SB_GROUP == 0 else 1
    n_units = group * n_pairs
    qkv3 = qkv.reshape(batch, s, n3)
    out = pl.pallas_call(
        _sb_kernel,
        grid=(batch // group, nq),
        in_specs=[
            pl.BlockSpec((group, SB_TQ, n), lambda b, i: (b, i, 0)),
            pl.BlockSpec((group, s, n), lambda b, i: (b, 0, 1)),
            pl.BlockSpec((group, s, n), lambda b, i: (b, 0, 2)),
            pl.BlockSpec((4 * SB_TK, 2 * SB_TK), lambda b, i: (0, 0)),
            pl.BlockSpec((1, n), lambda b, i: (0, 0)),
        ],
        out_specs=pl.BlockSpec((group, SB_TQ, n), lambda b, i: (b, i, 0)),
        out_shape=jax.ShapeDtypeStruct((batch, s, n), BF16),
        scratch_shapes=[
            pltpu.VMEM((n_units, SB_TQ, LANES), F32),
            pltpu.VMEM((2 * n_units, SB_TQ, LANES), F32),
            pltpu.VMEM((n_units, SB_TQ, 2 * SB_TK), F32),
            pltpu.VMEM((n_units, SB_TQ, 4 * SB_TK), BF16),
            pltpu.VMEM((n_units, SB_TQ, 2 * SB_TK), F32),
            pltpu.VMEM((n_units, SB_TQ, 2 * SB_TK), BF16),
        ],
        compiler_params=_cparams(("parallel", "arbitrary")),
        name="sb_attn",
    )(qkv3, qkv3, qkv3, u2, g)
    return out.reshape(t, n)


def _memkv_kernel(m_ref, g_ref, wk_ref, wv_ref, k_ref, v_ref):
    mn = _rms(m_ref[...], g_ref[...]).astype(BF16)
    k_ref[...] = _dot(mn, wk_ref[...]).astype(BF16)
    v_ref[...] = _dot(mn, wv_ref[...]).astype(BF16)


def _mem_kv(mem2d, g, wk, wv, batch):
    tmem, d = mem2d.shape
    m = tmem // batch
    n = wk.shape[1]
    const = lambda b: (0, 0)
    row = lambda b: (b, 0)
    return pl.pallas_call(
        _memkv_kernel,
        grid=(batch,),
        in_specs=[
            pl.BlockSpec((m, d), row),
            pl.BlockSpec((1, d), const),
            pl.BlockSpec((d, n), const),
            pl.BlockSpec((d, n), const),
        ],
        out_specs=[pl.BlockSpec((m, n), row), pl.BlockSpec((m, n), row)],
        out_shape=[jax.ShapeDtypeStruct((tmem, n), BF16)] * 2,
        compiler_params=_cparams(("parallel",)),
        name="mem_kv",
    )(mem2d, g, wk, wv)


def _tail_kernel(h_ref, ya_ref, yb_ref, wa_ref, wb_ref, gx_ref, wq_ref, k_ref, v_ref, wo_ref, gf_ref, w1_ref, w2_ref,
                 gfin_ref, o_ref, *, ff_chunk, final_norm):
    h = h_ref[...] + _dot(ya_ref[...], wa_ref[...]) + _dot(yb_ref[...], wb_ref[...])

    hn = _rms(h, gx_ref[...]).astype(BF16)
    q = _dot(hn, wq_ref[...]).astype(BF16)
    scale = 1.0 / math.sqrt(XA_HEAD_DIM)
    outs = []
    for hd in range(q.shape[1] // XA_HEAD_DIM):
        cs = slice(hd * XA_HEAD_DIM, (hd + 1) * XA_HEAD_DIM)
        logits = _dot_nt(q[:, cs], k_ref[:, cs]) * scale
        e = jnp.exp(logits - jnp.max(logits, axis=-1, keepdims=True))
        denom = jnp.sum(e, axis=-1, keepdims=True)
        outs.append(_dot(e.astype(BF16), v_ref[:, cs]) / denom)
    h = h + _dot(jnp.concatenate(outs, axis=1).astype(BF16), wo_ref[...])

    hn = _rms(h, gf_ref[...]).astype(BF16)
    out = h
    for c in range(w1_ref.shape[1] // ff_chunk):
        cs = slice(c * ff_chunk, (c + 1) * ff_chunk)
        u = jnp.maximum(_dot(hn, w1_ref[:, cs]), 0.0)
        out = out + _dot((u * u).astype(BF16), w2_ref[cs, :])
    if final_norm:
        out = _rms(out, gfin_ref[...])
    o_ref[...] = out


def _layer_tail(h, ya, yb, wa, wb, gx, wq, k, v, wo, gf, w1, w2, gfin, batch, final_norm, tm=512, ff_chunk=1024):
    t, d = h.shape
    na, nb = ya.shape[1], yb.shape[1]
    n = wq.shape[1]
    f = w1.shape[1]
    m = k.shape[0] // batch
    tiles_per_batch = t // batch // tm
    row = lambda i: (i, 0)
    kv = lambda i: (i // tiles_per_batch, 0)
    resident = lambda shape: pl.BlockSpec(shape, lambda i: (0, 0), pipeline_mode=pl.Buffered(1))
    return pl.pallas_call(
        functools.partial(_tail_kernel, ff_chunk=ff_chunk, final_norm=final_norm),
        grid=(t // tm,),
        in_specs=[
            pl.BlockSpec((tm, d), row),
            pl.BlockSpec((tm, na), row),
            pl.BlockSpec((tm, nb), row),
            resident((na, d)),
            resident((nb, d)),
            resident((1, d)),
            resident((d, n)),
            pl.BlockSpec((m, n), kv),
            pl.BlockSpec((m, n), kv),
            resident((n, d)),
            resident((1, d)),
            resident((d, f)),
            resident((f, d)),
            resident((1, d)),
        ],
        out_specs=pl.BlockSpec((tm, d), row),
        out_shape=jax.ShapeDtypeStruct((t, d), F32),
        compiler_params=_cparams(("parallel",)),
        name="layer_tail",
    )(h, ya, yb, wa, wb, gx, wq, k, v, wo, gf, w1, w2, gfin)


def _pad_lanes(v):
    return jnp.pad(v.astype(F32), (0, LANES - v.shape[0])).reshape(1, LANES)


def kernel(x, mem, norm_mix_g, w_in, conv_w, conv_b, dt_bias, a_log, d_skip, ssd_norm_g, sb_norm_g, w_out,
           norm_xa_g, norm_mem_g, w_xq, w_xk, w_xv, w_xo, norm_ff_g, w_ff1, w_ff2, final_g):
    batch, seq, d = x.shape
    depth = w_in.shape[0]
    n_heads = dt_bias.shape[1]
    n_inner = n_heads * SSD_HEAD_DIM
    n_conv = n_inner + 2 * SSD_GROUPS * SSD_STATE
    n_sb = (w_in.shape[2] - n_inner - n_conv - n_heads) // 3
    t = batch * seq
    row = lambda v: v.astype(F32).reshape(1, -1)

    jj = lax.broadcasted_iota(jnp.int32, (2 * SB_TK, 2 * SB_TK), 0)
    ss = lax.broadcasted_iota(jnp.int32, (2 * SB_TK, 2 * SB_TK), 1)
    u2 = -((jj // SB_TK == ss // SB_TK) & (jj >= ss)).astype(BF16)
    u2 = jnp.concatenate([u2, u2], axis=0)

    h = x.reshape(t, d)
    mem2d = mem.reshape(batch * mem.shape[1], d)
    for l in range(depth):
        o_dt = n_inner + n_conv
        o_q = o_dt + n_heads
        w_zx = w_in[l, :, :o_dt].astype(BF16)
        w_dt = jnp.pad(w_in[l, :, o_dt:o_q], ((0, 0), (0, LANES - n_heads))).astype(BF16)
        q_scale = 1.0 / math.sqrt(SB_HEAD_DIM)
        w_qkv = jnp.concatenate([w_in[l, :, o_q:o_q + n_sb] * q_scale, w_in[l, :, o_q + n_sb:]], axis=1).astype(BF16)

        zx, dt, qkv = _in_proj(h, row(norm_mix_g[l]), w_zx, w_dt, w_qkv)
        y_ssd = _ssd(zx, dt, conv_w[l].astype(F32)[:, None, :], row(conv_b[l]), _pad_lanes(dt_bias[l]), _pad_lanes(a_log[l]),
                     row(jnp.repeat(d_skip[l], SSD_HEAD_DIM)), row(ssd_norm_g[l]), batch)
        y_sb = _sb_attention(qkv, u2, row(sb_norm_g[l]), batch)
        k_mem, v_mem = _mem_kv(mem2d, row(norm_mem_g[l]), w_xk[l].astype(BF16), w_xv[l].astype(BF16), batch)
        h = _layer_tail(h, y_ssd, y_sb, w_out[l, :n_inner].astype(BF16), w_out[l, n_inner:].astype(BF16),
                        row(norm_xa_g[l]), w_xq[l].astype(BF16), k_mem, v_mem, w_xo[l].astype(BF16),
                        row(norm_ff_g[l]), w_ff1[l].astype(BF16), w_ff2[l].astype(BF16), row(final_g),
                        batch, final_norm=(l == depth - 1))
    return h.reshape(batch, seq, d)
```

```python
import functools
import math

import jax
import jax.numpy as jnp
from jax import lax
from jax.experimental import pallas as pl
from jax.experimental.pallas import tpu as pltpu

F32 = jnp.float32
BF16 = jnp.bfloat16

EPS = 1e-5
CONV_K = 4
CHUNK = 128
SSD_HEAD_DIM = 64
SSD_GROUPS = 2
SSD_STATE = 64
SB_HEAD_DIM = 64
XA_HEAD_DIM = 128

LANES = 128
SUBLANES = 8
MXU_N = 256
VMEM_LIMIT = 56 * 1024 * 1024

SB_TQ = 128
SB_TK = 128
SB_GROUP = 2
SB_SKIP_BELOW = -110.0
NEG_BIG = -1e30
LOG2_E = math.log2(math.e)


def _cparams(sem):
    return pltpu.CompilerParams(dimension_semantics=sem, vmem_limit_bytes=VMEM_LIMIT)


def _rms(x, g):
    return x * lax.rsqrt(jnp.mean(x * x, axis=-1, keepdims=True) + EPS) * g


def _softplus(x):
    return jnp.maximum(x, 0.0) + jnp.log(1.0 + jnp.exp(-jnp.abs(x)))


def _dot(a, b):
    return jnp.dot(a, b, preferred_element_type=F32)


def _dot_nt(a, b):
    return lax.dot_general(a, b, (((1,), (1,)), ((), ())), preferred_element_type=F32)


def _split_bf16(x, parts):
    out = []
    r = x
    for _ in range(parts - 1):
        p = r.astype(BF16)
        out.append(p)
        r = r - p.astype(F32)
    out.append(r.astype(BF16))
    return out


def _expand_heads(a, lane):
    cols = []
    for j in range(4):
        lo = jnp.broadcast_to(a[:, 2 * j:2 * j + 1], (a.shape[0], LANES))
        hi = jnp.broadcast_to(a[:, 2 * j + 1:2 * j + 2], (a.shape[0], LANES))
        cols.append(jnp.where(lane < SSD_HEAD_DIM, lo, hi))
    return jnp.concatenate(cols, axis=1)


def _ssd_chunk(z, window, dt_raw, st, cw_ref, cb_ref, dtb_ref, alog_ref, dsk_ref, gn_ref, interleave):
    n_inner = z.shape[1]
    halo = window.shape[0] - CHUNK
    interleave = list(interleave)
    per_stage = -(-len(interleave) // 3)

    def run_interleaved(stage):
        for job in interleave[stage * per_stage:(stage + 1) * per_stage]:
            job()

    run_interleaved(0)

    conv = cb_ref[...] + cw_ref[CONV_K - 1] * window[halo:, :]
    for k in range(CONV_K - 1):
        shifted = pltpu.roll(window, CONV_K - 1 - k, 0)[halo:, :]
        conv = conv + cw_ref[k] * shifted
    xbc = conv * jax.nn.sigmoid(conv)
    xs = xbc[:, :n_inner]
    bm = xbc[:, n_inner:n_inner + LANES]
    cm = xbc[:, n_inner + LANES:]

    row = lax.broadcasted_iota(jnp.int32, (CHUNK, CHUNK), 0)
    col = lax.broadcasted_iota(jnp.int32, (CHUNK, CHUNK), 1)
    causal = row >= col
    lane = col

    dt = _softplus(dt_raw + dtb_ref[...])
    a_c = dt * (-jnp.exp(alog_ref[...]))
    tri = jnp.where(causal, 1.0, 0.0).astype(BF16)
    a_cum = None
    for part in _split_bf16(a_c, 3):
        term = _dot(tri, part)
        a_cum = term if a_cum is None else a_cum + term
    a_cum_t = a_cum.T

    dt_x = _expand_heads(dt, lane)
    acum_x = _expand_heads(a_cum, lane)
    last_x = acum_x[CHUNK - 1:CHUNK, :]
    xc = xs * dt_x
    e_acum = jnp.exp(acum_x)
    xd = (xc * jnp.exp(last_x - acum_x)).astype(BF16)
    chunk_decay = jnp.exp(last_x)
    run_interleaved(1)

    bm_b = bm.astype(BF16)
    cbs = []
    for g in range(SSD_GROUPS):
        in_g = (lane >= g * SSD_STATE) & (lane < (g + 1) * SSD_STATE)
        cbs.append(_dot_nt(jnp.where(in_g, cm, 0.0).astype(BF16), bm_b))
    y_cols = []
    for j in range(n_inner // LANES):
        ms = []
        for h in (2 * j, 2 * j + 1):
            seg = a_cum[:, h:h + 1] - a_cum_t[h:h + 1, :]
            decay = jnp.exp(jnp.where(causal, seg, NEG_BIG))
            ms.append((cbs[h * SSD_HEAD_DIM * SSD_GROUPS // n_inner] * decay).astype(BF16))
        xcj = xc[:, j * LANES:(j + 1) * LANES]
        x_lo = jnp.where(lane < SSD_HEAD_DIM, xcj, 0.0).astype(BF16)
        x_hi = jnp.where(lane >= SSD_HEAD_DIM, xcj, 0.0).astype(BF16)
        y_cols.append(_dot(jnp.concatenate(ms, axis=1), jnp.concatenate([x_lo, x_hi], axis=0)))
    y = jnp.concatenate(y_cols, axis=1)
    run_interleaved(2)

    y = y + _dot(cm.astype(BF16), st.astype(BF16)) * e_acum
    srow = lax.broadcasted_iota(jnp.int32, st.shape, 0)
    scol = lax.broadcasted_iota(jnp.int32, st.shape, 1)
    same_group = (srow // SSD_STATE) == (scol // (n_inner // SSD_GROUPS))
    new_st = st * chunk_decay + jnp.where(same_group, _dot(bm.T.astype(BF16), xd), 0.0)

    y = y + dsk_ref[...] * xs
    y = y * (z * jax.nn.sigmoid(z))
    return _rms(y, gn_ref[...]).astype(BF16), new_st


def _proj_ssd_kernel(h_ref, g_ref, wzx_ref, wdt_ref, wqkv_ref, cw_ref, cb_ref, dtb_ref, alog_ref, dsk_ref, gn_ref,
                     qkv_ref, y_ref, zx_a, dt_a, zx_b, dt_b, halo_scr, state, *, tiles_per_batch):
    s = pl.program_id(0)
    tm = h_ref.shape[0]
    n_inner = y_ref.shape[1]

    @pl.when(s == 0)
    def _():
        zx_b[...] = jnp.zeros(zx_b.shape, F32)
        dt_b[...] = jnp.zeros(dt_b.shape, F32)
        halo_scr[...] = jnp.zeros(halo_scr.shape, F32)
        state[...] = jnp.zeros(state.shape, F32)

    def step(zx_w, dt_w, zx_r, dt_r):
        hn = _rms(h_ref[...], g_ref[...]).astype(BF16)
        n_chunks = tm // CHUNK
        nzx, nqkv = zx_w.shape[1], qkv_ref.shape[1]

        def project(out_ref, w_ref, a):
            b = min(a + MXU_N, w_ref.shape[1])
            out_ref[:, a:b] = _dot(hn, w_ref[:, a:b]).astype(out_ref.dtype)

        jobs = [functools.partial(project, zx_w, wzx_ref, a) for a in range(0, nzx, MXU_N)]
        jobs += [functools.partial(project, dt_w, wdt_ref, 0)]
        jobs += [functools.partial(project, qkv_ref, wqkv_ref, a) for a in range(0, nqkv, MXU_N)]
        share = -(-len(jobs) // n_chunks)

        fresh = lax.rem(s - 1, tiles_per_batch) == 0
        st = jnp.where(fresh, 0.0, state[...])
        halo = jnp.where(fresh, 0.0, halo_scr[...])
        for c in range(n_chunks):
            r0 = c * CHUNK
            if c == 0:
                window = jnp.concatenate([halo, zx_r[0:CHUNK, n_inner:]], axis=0)
            else:
                window = zx_r[r0 - SUBLANES:r0 + CHUNK, n_inner:]
            y, st = _ssd_chunk(zx_r[r0:r0 + CHUNK, :n_inner], window, dt_r[r0:r0 + CHUNK, :], st,
                               cw_ref, cb_ref, dtb_ref, alog_ref, dsk_ref, gn_ref, jobs[c * share:(c + 1) * share])
            y_ref[r0:r0 + CHUNK, :] = y
        state[...] = st
        halo_scr[...] = zx_r[tm - SUBLANES:tm, n_inner:]

    @pl.when(s % 2 == 0)
    def _():
        step(zx_a, dt_a, zx_b, dt_b)

    @pl.when(s % 2 == 1)
    def _():
        step(zx_b, dt_b, zx_a, dt_a)


def _proj_ssd(h, g, wzx, wdt, wqkv, cw, cb, dtb, alog, dsk, gn, batch, tm=512):
    t, d = h.shape
    nzx, ndt, nqkv = wzx.shape[1], wdt.shape[1], wqkv.shape[1]
    n_inner = gn.shape[1]
    n_conv = nzx - n_inner
    n_tiles = t // tm
    last = n_tiles - 1
    resident = lambda shape: pl.BlockSpec(shape, lambda s: (0,) * len(shape), pipeline_mode=pl.Buffered(1))
    return pl.pallas_call(
        functools.partial(_proj_ssd_kernel, tiles_per_batch=n_tiles // batch),
        grid=(n_tiles + 1,),
        in_specs=[
            pl.BlockSpec((tm, d), lambda s: (jnp.minimum(s, last), 0)),
            resident((1, d)),
            resident((d, nzx)),
            resident((d, ndt)),
            resident((d, nqkv)),
            resident((CONV_K, 1, n_conv)),
            resident((1, n_conv)),
            resident((1, LANES)),
            resident((1, LANES)),
            resident((1, n_inner)),
            resident((1, n_inner)),
        ],
        out_specs=[
            pl.BlockSpec((tm, nqkv), lambda s: (jnp.minimum(s, last), 0)),
            pl.BlockSpec((tm, n_inner), lambda s: (jnp.maximum(s - 1, 0), 0)),
        ],
        out_shape=[
            jax.ShapeDtypeStruct((t, nqkv), BF16),
            jax.ShapeDtypeStruct((t, n_inner), BF16),
        ],
        scratch_shapes=[
            pltpu.VMEM((tm, nzx), F32),
            pltpu.VMEM((tm, ndt), F32),
            pltpu.VMEM((tm, nzx), F32),
            pltpu.VMEM((tm, ndt), F32),
            pltpu.VMEM((SUBLANES, n_conv), F32),
            pltpu.VMEM((SSD_GROUPS * SSD_STATE, n_inner), F32),
        ],
        compiler_params=_cparams(("arbitrary",)),
        name="proj_ssd",
    )(h, g, wzx, wdt, wqkv, cw, cb, dtb, alog, dsk, gn)


def _sb_kernel(q_ref, k_ref, v_ref, u2_ref, g_ref, o_ref, acc, rsum, z_scr, hl_scr, s_scr, w_scr):
    n_group = q_ref.shape[0]
    n_pairs = q_ref.shape[2] // LANES
    units = [(g, p) for g in range(n_group) for p in range(n_pairs)]
    qi = pl.program_id(1)
    acc[...] = jnp.zeros(acc.shape, F32)
    rsum[...] = jnp.zeros(rsum.shape, F32)

    row2 = lax.broadcasted_iota(jnp.int32, (SB_TQ, 2 * SB_TK), 0)
    col2 = lax.broadcasted_iota(jnp.int32, (SB_TQ, 2 * SB_TK), 1)
    below = (col2 & (SB_TK - 1)) < row2
    lane = lax.broadcasted_iota(jnp.int32, (SB_TK, LANES), 1)
    lo_m = jnp.where(lane < SB_HEAD_DIM, 1.0, 0.0).astype(BF16)
    hi_m = jnp.where(lane >= SB_HEAD_DIM, 1.0, 0.0).astype(BF16)

    def walk_block(j, diagonal):
        off = pl.multiple_of(j * SB_TK, SB_TK)
        for u, (g, p) in enumerate(units):
            cs = slice(p * LANES, (p + 1) * LANES)
            kb = k_ref[g, pl.ds(off, SB_TK), cs]
            k2 = jnp.concatenate([kb * lo_m, kb * hi_m], axis=0)
            z = _dot_nt(q_ref[g, :, cs], k2)
            z_scr[u] = z * LOG2_E
        for u in range(len(units)):
            z2 = z_scr[u]
            sp2 = jnp.maximum(z2, 0.0) + jnp.log2(1.0 + jnp.exp2(-jnp.abs(z2)))
            if diagonal:
                sp2 = jnp.where(below, sp2, 0.0)
                z_scr[u] = jnp.where(below, z2, NEG_BIG)
            hi, lo = _split_bf16(sp2, 2)
            hl_scr[u, :, 0:2 * SB_TK] = hi
            hl_scr[u, :, 2 * SB_TK:4 * SB_TK] = lo
        for u in range(len(units)):
            s_scr[u] = _dot(hl_scr[u], u2_ref[...])
        rmax = None
        for u in range(len(units)):
            r_prev = jnp.concatenate([rsum[2 * u], rsum[2 * u + 1]], axis=1)
            s = s_scr[u] + r_prev
            w_scr[u] = jnp.exp2(z_scr[u] + s).astype(BF16)
            for e in range(2):
                r_new = jnp.broadcast_to(s[:, e * SB_TK:e * SB_TK + 1], (SB_TQ, LANES))
                rsum[2 * u + e] = r_new
                rmax = r_new if rmax is None else jnp.maximum(rmax, r_new)
        for u, (g, p) in enumerate(units):
            cs = slice(p * LANES, (p + 1) * LANES)
            vb = v_ref[g, pl.ds(off, SB_TK), cs]
            v2 = jnp.concatenate([vb * lo_m, vb * hi_m], axis=0)
            acc[u] += _dot(w_scr[u], v2)
        return jnp.max(rmax)

    def cond(carry):
        j, rmax = carry
        return jnp.logical_and(j >= 0, rmax > SB_SKIP_BELOW * LOG2_E)

    def body(carry):
        j, _ = carry
        return j - 1, walk_block(j, False)

    lax.while_loop(cond, body, (qi - 1, walk_block(qi, True)))

    for g in range(n_group):
        o = jnp.concatenate([acc[g * n_pairs + p] for p in range(n_pairs)], axis=1)
        o_ref[g] = _rms(o, g_ref[...]).astype(BF16)


def _sb_attention(qkv, u2, g, batch):
    t, n3 = qkv.shape
    n = n3 // 3
    s = t // batch
    nq = s // SB_TQ
    n_pairs = n // LANES
    group = SB_GROUP if batch % SB_GROUP == 0 else 1
    n_units = group * n_pairs
    qkv3 = qkv.reshape(batch, s, n3)
    out = pl.pallas_call(
        _sb_kernel,
        grid=(batch // group, nq),
        in_specs=[
            pl.BlockSpec((group, SB_TQ, n), lambda b, i: (b, i, 0)),
            pl.BlockSpec((group, s, n), lambda b, i: (b, 0, 1)),
            pl.BlockSpec((group, s, n), lambda b, i: (b, 0, 2)),
            pl.BlockSpec((4 * SB_TK, 2 * SB_TK), lambda b, i: (0, 0)),
            pl.BlockSpec((1, n), lambda b, i: (0, 0)),
        ],
        out_specs=pl.BlockSpec((group, SB_TQ, n), lambda b, i: (b, i, 0)),
        out_shape=jax.ShapeDtypeStruct((batch, s, n), BF16),
        scratch_shapes=[
            pltpu.VMEM((n_units, SB_TQ, LANES), F32),
            pltpu.VMEM((2 * n_units, SB_TQ, LANES), F32),
            pltpu.VMEM((n_units, SB_TQ, 2 * SB_TK), F32),
            pltpu.VMEM((n_units, SB_TQ, 4 * SB_TK), BF16),
            pltpu.VMEM((n_units, SB_TQ, 2 * SB_TK), F32),
            pltpu.VMEM((n_units, SB_TQ, 2 * SB_TK), BF16),
        ],
        compiler_params=_cparams(("parallel", "arbitrary")),
        name="sb_attn",
    )(qkv3, qkv3, qkv3, u2, g)
    return out.reshape(t, n)


def _memkv_kernel(m_ref, g_ref, wk_ref, wv_ref, k_ref, v_ref):
    mn = _rms(m_ref[...], g_ref[...]).astype(BF16)
    k_ref[...] = _dot(mn, wk_ref[...]).astype(BF16)
    v_ref[...] = _dot(mn, wv_ref[...]).astype(BF16)


def _mem_kv(mem2d, g, wk, wv, batch):
    tmem, d = mem2d.shape
    m = tmem // batch
    n = wk.shape[1]
    const = lambda b: (0, 0)
    row = lambda b: (b, 0)
    return pl.pallas_call(
        _memkv_kernel,
        grid=(batch,),
        in_specs=[
            pl.BlockSpec((m, d), row),
            pl.BlockSpec((1, d), const),
            pl.BlockSpec((d, n), const),
            pl.BlockSpec((d, n), const),
        ],
        out_specs=[pl.BlockSpec((m, n), row), pl.BlockSpec((m, n), row)],
        out_shape=[jax.ShapeDtypeStruct((tmem, n), BF16)] * 2,
        compiler_params=_cparams(("parallel",)),
        name="mem_kv",
    )(mem2d, g, wk, wv)


def _tail_kernel(h_ref, ya_ref, yb_ref, wa_ref, wb_ref, gx_ref, wq_ref, k_ref, v_ref, wo_ref, gf_ref, w1_ref, w2_ref,
                 gfin_ref, o_ref, *, ff_chunk, final_norm):
    h = h_ref[...] + _dot(ya_ref[...], wa_ref[...]) + _dot(yb_ref[...], wb_ref[...])

    hn = _rms(h, gx_ref[...]).astype(BF16)
    q = _dot(hn, wq_ref[...]).astype(BF16)
    scale = 1.0 / math.sqrt(XA_HEAD_DIM)
    outs = []
    for hd in range(q.shape[1] // XA_HEAD_DIM):
        cs = slice(hd * XA_HEAD_DIM, (hd + 1) * XA_HEAD_DIM)
        logits = _dot_nt(q[:, cs], k_ref[:, cs]) * scale
        e = jnp.exp(logits - jnp.max(logits, axis=-1, keepdims=True))
        denom = jnp.sum(e, axis=-1, keepdims=True)
        outs.append(_dot(e.astype(BF16), v_ref[:, cs]) / denom)
    h = h + _dot(jnp.concatenate(outs, axis=1).astype(BF16), wo_ref[...])

    hn = _rms(h, gf_ref[...]).astype(BF16)
    out = h
    for c in range(w1_ref.shape[1] // ff_chunk):
        cs = slice(c * ff_chunk, (c + 1) * ff_chunk)
        u = jnp.maximum(_dot(hn, w1_ref[:, cs]), 0.0)
        out = out + _dot((u * u).astype(BF16), w2_ref[cs, :])
    if final_norm:
        out = _rms(out, gfin_ref[...])
    o_ref[...] = out


def _layer_tail(h, ya, yb, wa, wb, gx, wq, k, v, wo, gf, w1, w2, gfin, batch, final_norm, tm=512, ff_chunk=1024):
    t, d = h.shape
    na, nb = ya.shape[1], yb.shape[1]
    n = wq.shape[1]
    f = w1.shape[1]
    m = k.shape[0] // batch
    tiles_per_batch = t // batch // tm
    row = lambda i: (i, 0)
    kv = lambda i: (i // tiles_per_batch, 0)
    resident = lambda shape: pl.BlockSpec(shape, lambda i: (0, 0), pipeline_mode=pl.Buffered(1))
    return pl.pallas_call(
        functools.partial(_tail_kernel, ff_chunk=ff_chunk, final_norm=final_norm),
        grid=(t // tm,),
        in_specs=[
            pl.BlockSpec((tm, d), row),
            pl.BlockSpec((tm, na), row),
            pl.BlockSpec((tm, nb), row),
            resident((na, d)),
            resident((nb, d)),
            resident((1, d)),
            resident((d, n)),
            pl.BlockSpec((m, n), kv),
            pl.BlockSpec((m, n), kv),
            resident((n, d)),
            resident((1, d)),
            resident((d, f)),
            resident((f, d)),
            resident((1, d)),
        ],
        out_specs=pl.BlockSpec((tm, d), row),
        out_shape=jax.ShapeDtypeStruct((t, d), F32),
        compiler_params=_cparams(("parallel",)),
        name="layer_tail",
    )(h, ya, yb, wa, wb, gx, wq, k, v, wo, gf, w1, w2, gfin)


def _pad_lanes(v):
    return jnp.pad(v.astype(F32), (0, LANES - v.shape[0])).reshape(1, LANES)


def kernel(x, mem, norm_mix_g, w_in, conv_w, conv_b, dt_bias, a_log, d_skip, ssd_norm_g, sb_norm_g, w_out,
           norm_xa_g, norm_mem_g, w_xq, w_xk, w_xv, w_xo, norm_ff_g, w_ff1, w_ff2, final_g):
    batch, seq, d = x.shape
    depth = w_in.shape[0]
    n_heads = dt_bias.shape[1]
    n_inner = n_heads * SSD_HEAD_DIM
    n_conv = n_inner + 2 * SSD_GROUPS * SSD_STATE
    n_sb = (w_in.shape[2] - n_inner - n_conv - n_heads) // 3
    t = batch * seq
    row = lambda v: v.astype(F32).reshape(1, -1)

    jj = lax.broadcasted_iota(jnp.int32, (2 * SB_TK, 2 * SB_TK), 0)
    ss = lax.broadcasted_iota(jnp.int32, (2 * SB_TK, 2 * SB_TK), 1)
    u2 = -((jj // SB_TK == ss // SB_TK) & (jj >= ss)).astype(BF16)
    u2 = jnp.concatenate([u2, u2], axis=0)

    h = x.reshape(t, d)
    mem2d = mem.reshape(batch * mem.shape[1], d)
    for l in range(depth):
        o_dt = n_inner + n_conv
        o_q = o_dt + n_heads
        w_zx = w_in[l, :, :o_dt].astype(BF16)
        w_dt = jnp.pad(w_in[l, :, o_dt:o_q], ((0, 0), (0, LANES - n_heads))).astype(BF16)
        q_scale = 1.0 / math.sqrt(SB_HEAD_DIM)
        w_qkv = jnp.concatenate([w_in[l, :, o_q:o_q + n_sb] * q_scale, w_in[l, :, o_q + n_sb:]], axis=1).astype(BF16)

        qkv, y_ssd = _proj_ssd(h, row(norm_mix_g[l]), w_zx, w_dt, w_qkv, conv_w[l].astype(F32)[:, None, :],
                               row(conv_b[l]), _pad_lanes(dt_bias[l]), _pad_lanes(a_log[l]),
                               row(jnp.repeat(d_skip[l], SSD_HEAD_DIM)), row(ssd_norm_g[l]), batch)
        y_sb = _sb_attention(qkv, u2, row(sb_norm_g[l]), batch)
        k_mem, v_mem = _mem_kv(mem2d, row(norm_mem_g[l]), w_xk[l].astype(BF16), w_xv[l].astype(BF16), batch)
        h = _layer_tail(h, y_ssd, y_sb, w_out[l, :n_inner].astype(BF16), w_out[l, n_inner:].astype(BF16),
                        row(norm_xa_g[l]), w_xq[l].astype(BF16), k_mem, v_mem, w_xo[l].astype(BF16),
                        row(norm_ff_g[l]), w_ff1[l].astype(BF16), w_ff2[l].astype(BF16), row(final_g),
                        batch, final_norm=(l == depth - 1))
    return h.reshape(batch, seq, d)
```

```python
import functools
import math

import jax
import jax.numpy as jnp
from jax import lax
from jax.experimental import pallas as pl
from jax.experimental.pallas import tpu as pltpu

F32 = jnp.float32
BF16 = jnp.bfloat16

EPS = 1e-5
CONV_K = 4
CHUNK = 128
SSD_HEAD_DIM = 64
SSD_GROUPS = 2
SSD_STATE = 64
SB_HEAD_DIM = 64
XA_HEAD_DIM = 128

LANES = 128
SUBLANES = 8
MXU_N = 256
VMEM_LIMIT = 56 * 1024 * 1024

SB_TQ = 128
SB_TK = 128
SB_GROUP = 2
SB_STATIC_BLOCKS = 3
SB_SKIP_BELOW = -110.0
NEG_BIG = -1e30
LOG2_E = math.log2(math.e)


def _cparams(sem):
    return pltpu.CompilerParams(dimension_semantics=sem, vmem_limit_bytes=VMEM_LIMIT)


def _rms(x, g):
    return x * lax.rsqrt(jnp.mean(x * x, axis=-1, keepdims=True) + EPS) * g


def _softplus(x):
    return jnp.maximum(x, 0.0) + jnp.log(1.0 + jnp.exp(-jnp.abs(x)))


def _dot(a, b):
    return jnp.dot(a, b, preferred_element_type=F32)


def _dot_nt(a, b):
    return lax.dot_general(a, b, (((1,), (1,)), ((), ())), preferred_element_type=F32)


def _split_bf16(x, parts):
    out = []
    r = x
    for _ in range(parts - 1):
        p = r.astype(BF16)
        out.append(p)
        r = r - p.astype(F32)
    out.append(r.astype(BF16))
    return out


def _expand_heads(a, lane):
    cols = []
    for j in range(4):
        lo = jnp.broadcast_to(a[:, 2 * j:2 * j + 1], (a.shape[0], LANES))
        hi = jnp.broadcast_to(a[:, 2 * j + 1:2 * j + 2], (a.shape[0], LANES))
        cols.append(jnp.where(lane < SSD_HEAD_DIM, lo, hi))
    return jnp.concatenate(cols, axis=1)


def _ssd_chunk(z, window, dt_raw, st, cw_ref, cb_ref, dtb_ref, alog_ref, dsk_ref, gn_ref, interleave):
    n_inner = z.shape[1]
    halo = window.shape[0] - CHUNK
    interleave = list(interleave)
    per_stage = -(-len(interleave) // 3)

    def run_interleaved(stage):
        for job in interleave[stage * per_stage:(stage + 1) * per_stage]:
            job()

    run_interleaved(0)

    conv = cb_ref[...] + cw_ref[CONV_K - 1] * window[halo:, :]
    for k in range(CONV_K - 1):
        shifted = pltpu.roll(window, CONV_K - 1 - k, 0)[halo:, :]
        conv = conv + cw_ref[k] * shifted
    xbc = conv * jax.nn.sigmoid(conv)
    xs = xbc[:, :n_inner]
    bm = xbc[:, n_inner:n_inner + LANES]
    cm = xbc[:, n_inner + LANES:]

    row = lax.broadcasted_iota(jnp.int32, (CHUNK, CHUNK), 0)
    col = lax.broadcasted_iota(jnp.int32, (CHUNK, CHUNK), 1)
    causal = row >= col
    lane = col

    dt = _softplus(dt_raw + dtb_ref[...])
    a_c = dt * (-jnp.exp(alog_ref[...]))
    tri = jnp.where(causal, 1.0, 0.0).astype(BF16)
    a_cum = None
    for part in _split_bf16(a_c, 3):
        term = _dot(tri, part)
        a_cum = term if a_cum is None else a_cum + term
    a_cum_t = a_cum.T

    dt_x = _expand_heads(dt, lane)
    acum_x = _expand_heads(a_cum, lane)
    last_x = acum_x[CHUNK - 1:CHUNK, :]
    xc = xs * dt_x
    e_acum = jnp.exp(acum_x)
    xd = (xc * jnp.exp(last_x - acum_x)).astype(BF16)
    chunk_decay = jnp.exp(last_x)
    run_interleaved(1)

    bm_b = bm.astype(BF16)
    cbs = []
    for g in range(SSD_GROUPS):
        in_g = (lane >= g * SSD_STATE) & (lane < (g + 1) * SSD_STATE)
        cbs.append(_dot_nt(jnp.where(in_g, cm, 0.0).astype(BF16), bm_b))
    y_cols = []
    for j in range(n_inner // LANES):
        ms = []
        for h in (2 * j, 2 * j + 1):
            seg = a_cum[:, h:h + 1] - a_cum_t[h:h + 1, :]
            decay = jnp.exp(jnp.where(causal, seg, NEG_BIG))
            ms.append((cbs[h * SSD_HEAD_DIM * SSD_GROUPS // n_inner] * decay).astype(BF16))
        xcj = xc[:, j * LANES:(j + 1) * LANES]
        x_lo = jnp.where(lane < SSD_HEAD_DIM, xcj, 0.0).astype(BF16)
        x_hi = jnp.where(lane >= SSD_HEAD_DIM, xcj, 0.0).astype(BF16)
        y_cols.append(_dot(jnp.concatenate(ms, axis=1), jnp.concatenate([x_lo, x_hi], axis=0)))
    y = jnp.concatenate(y_cols, axis=1)
    run_interleaved(2)

    y = y + _dot(cm.astype(BF16), st.astype(BF16)) * e_acum
    srow = lax.broadcasted_iota(jnp.int32, st.shape, 0)
    scol = lax.broadcasted_iota(jnp.int32, st.shape, 1)
    same_group = (srow // SSD_STATE) == (scol // (n_inner // SSD_GROUPS))
    new_st = st * chunk_decay + jnp.where(same_group, _dot(bm.T.astype(BF16), xd), 0.0)

    y = y + dsk_ref[...] * xs
    y = y * (z * jax.nn.sigmoid(z))
    return _rms(y, gn_ref[...]).astype(BF16), new_st


def _proj_ssd_kernel(h_ref, g_ref, win_ref, cw_ref, cb_ref, dtb_ref, alog_ref, dsk_ref, gn_ref,
                     qkv_ref, y_ref, wzx_ref, wdt_ref, wqkv_ref, zx_a, dt_a, zx_b, dt_b, halo_scr, state,
                     *, tiles_per_batch, n_heads):
    s = pl.program_id(0)
    tm = h_ref.shape[0]
    n_inner = y_ref.shape[1]

    @pl.when(s == 0)
    def _():
        zx_b[...] = jnp.zeros(zx_b.shape, F32)
        dt_b[...] = jnp.zeros(dt_b.shape, F32)
        halo_scr[...] = jnp.zeros(halo_scr.shape, F32)
        state[...] = jnp.zeros(state.shape, F32)
        nzx, nqkv = wzx_ref.shape[1], wqkv_ref.shape[1]
        n_sb = nqkv // 3
        q_scale = 1.0 / math.sqrt(SB_HEAD_DIM)
        lane = lax.broadcasted_iota(jnp.int32, (CHUNK, LANES), 1)
        for r in range(0, win_ref.shape[0], CHUNK):
            rows = slice(r, r + CHUNK)
            wzx_ref[rows, :] = win_ref[rows, 0:nzx].astype(BF16)
            rest = win_ref[rows, nzx:]
            wdt_ref[rows, :] = jnp.where(lane < n_heads, rest[:, 0:LANES], 0.0).astype(BF16)
            wqkv_ref[rows, 0:n_sb] = (rest[:, n_heads:n_heads + n_sb] * q_scale).astype(BF16)
            wqkv_ref[rows, n_sb:] = rest[:, n_heads + n_sb:].astype(BF16)

    def step(zx_w, dt_w, zx_r, dt_r):
        hn = _rms(h_ref[...], g_ref[...]).astype(BF16)
        n_chunks = tm // CHUNK
        nzx, nqkv = zx_w.shape[1], qkv_ref.shape[1]

        def project(out_ref, w_ref, a):
            b = min(a + MXU_N, w_ref.shape[1])
            out_ref[:, a:b] = _dot(hn, w_ref[:, a:b]).astype(out_ref.dtype)

        jobs = [functools.partial(project, zx_w, wzx_ref, a) for a in range(0, nzx, MXU_N)]
        jobs += [functools.partial(project, dt_w, wdt_ref, 0)]
        jobs += [functools.partial(project, qkv_ref, wqkv_ref, a) for a in range(0, nqkv, MXU_N)]
        share = -(-len(jobs) // n_chunks)

        fresh = lax.rem(s - 1, tiles_per_batch) == 0
        st = jnp.where(fresh, 0.0, state[...])
        halo = jnp.where(fresh, 0.0, halo_scr[...])
        for c in range(n_chunks):
            r0 = c * CHUNK
            if c == 0:
                window = jnp.concatenate([halo, zx_r[0:CHUNK, n_inner:]], axis=0)
            else:
                window = zx_r[r0 - SUBLANES:r0 + CHUNK, n_inner:]
            y, st = _ssd_chunk(zx_r[r0:r0 + CHUNK, :n_inner], window, dt_r[r0:r0 + CHUNK, :], st,
                               cw_ref, cb_ref, dtb_ref, alog_ref, dsk_ref, gn_ref, jobs[c * share:(c + 1) * share])
            y_ref[r0:r0 + CHUNK, :] = y
        state[...] = st
        halo_scr[...] = zx_r[tm - SUBLANES:tm, n_inner:]

    @pl.when(s % 2 == 0)
    def _():
        step(zx_a, dt_a, zx_b, dt_b)

    @pl.when(s % 2 == 1)
    def _():
        step(zx_b, dt_b, zx_a, dt_a)


def _proj_ssd(h, g, w_in, layer, cw, cb, dtb, alog, dsk, gn, batch, n_heads, tm=512):
    t, d = h.shape
    n_inner = gn.shape[1]
    n_conv = cb.shape[1]
    nzx, ndt = n_inner + n_conv, LANES
    in_dim = w_in.shape[2]
    nqkv = in_dim - nzx - n_heads
    n_tiles = t // tm
    last = n_tiles - 1
    resident = lambda shape: pl.BlockSpec(shape, lambda s: (0,) * len(shape), pipeline_mode=pl.Buffered(1))
    return pl.pallas_call(
        functools.partial(_proj_ssd_kernel, tiles_per_batch=n_tiles // batch, n_heads=n_heads),
        grid=(n_tiles + 1,),
        in_specs=[
            pl.BlockSpec((tm, d), lambda s: (jnp.minimum(s, last), 0)),
            resident((1, d)),
            pl.BlockSpec((None, d, in_dim), lambda s: (layer, 0, 0), pipeline_mode=pl.Buffered(1)),
            resident((CONV_K, 1, n_conv)),
            resident((1, n_conv)),
            resident((1, LANES)),
            resident((1, LANES)),
            resident((1, n_inner)),
            resident((1, n_inner)),
        ],
        out_specs=[
            pl.BlockSpec((tm, nqkv), lambda s: (jnp.minimum(s, last), 0)),
            pl.BlockSpec((tm, n_inner), lambda s: (jnp.maximum(s - 1, 0), 0)),
        ],
        out_shape=[
            jax.ShapeDtypeStruct((t, nqkv), BF16),
            jax.ShapeDtypeStruct((t, n_inner), BF16),
        ],
        scratch_shapes=[
            pltpu.VMEM((d, nzx), BF16),
            pltpu.VMEM((d, ndt), BF16),
            pltpu.VMEM((d, nqkv), BF16),
            pltpu.VMEM((tm, nzx), F32),
            pltpu.VMEM((tm, ndt), F32),
            pltpu.VMEM((tm, nzx), F32),
            pltpu.VMEM((tm, ndt), F32),
            pltpu.VMEM((SUBLANES, n_conv), F32),
            pltpu.VMEM((SSD_GROUPS * SSD_STATE, n_inner), F32),
        ],
        compiler_params=_cparams(("arbitrary",)),
        name="proj_ssd",
    )(h, g, w_in, cw, cb, dtb, alog, dsk, gn)


def _sb_kernel(q_ref, k_ref, v_ref, u2_ref, g_ref, o_ref, acc, rsum, z_scr, hl_scr, s_scr, w_scr):
    n_group = q_ref.shape[0]
    n_pairs = q_ref.shape[2] // LANES
    units = [(g, p) for g in range(n_group) for p in range(n_pairs)]
    qi = pl.program_id(1)
    acc[...] = jnp.zeros(acc.shape, F32)
    rsum[...] = jnp.zeros(rsum.shape, F32)

    row2 = lax.broadcasted_iota(jnp.int32, (SB_TQ, 2 * SB_TK), 0)
    col2 = lax.broadcasted_iota(jnp.int32, (SB_TQ, 2 * SB_TK), 1)
    below = (col2 & (SB_TK - 1)) < row2
    lane = lax.broadcasted_iota(jnp.int32, (SB_TK, LANES), 1)
    lo_m = jnp.where(lane < SB_HEAD_DIM, 1.0, 0.0).astype(BF16)
    hi_m = jnp.where(lane >= SB_HEAD_DIM, 1.0, 0.0).astype(BF16)

    n_units = len(units)

    def block_front(j, diagonal, slot):
        off = pl.multiple_of(j * SB_TK, SB_TK)
        base = slot * n_units
        for u, (g, p) in enumerate(units):
            cs = slice(p * LANES, (p + 1) * LANES)
            kb = k_ref[g, pl.ds(off, SB_TK), cs]
            k2 = jnp.concatenate([kb * lo_m, kb * hi_m], axis=0)
            z = _dot_nt(q_ref[g, :, cs], k2)
            z_scr[base + u] = z * LOG2_E
        for u in range(n_units):
            z2 = z_scr[base + u]
            sp2 = jnp.maximum(z2, 0.0) + jnp.log2(1.0 + jnp.exp2(-jnp.abs(z2)))
            if diagonal:
                sp2 = jnp.where(below, sp2, 0.0)
                z_scr[base + u] = jnp.where(below, z2, NEG_BIG)
            hi, lo = _split_bf16(sp2, 2)
            hl_scr[base + u, :, 0:2 * SB_TK] = hi
            hl_scr[base + u, :, 2 * SB_TK:4 * SB_TK] = lo
        for u in range(n_units):
            s_scr[base + u] = _dot(hl_scr[base + u], u2_ref[...])

    def block_back(j, slot):
        off = pl.multiple_of(j * SB_TK, SB_TK)
        base = slot * n_units
        rmax = None
        for u in range(n_units):
            r_prev = jnp.concatenate([rsum[2 * u], rsum[2 * u + 1]], axis=1)
            s = s_scr[base + u] + r_prev
            w_scr[base + u] = jnp.exp2(z_scr[base + u] + s).astype(BF16)
            for e in range(2):
                r_new = jnp.broadcast_to(s[:, e * SB_TK:e * SB_TK + 1], (SB_TQ, LANES))
                rsum[2 * u + e] = r_new
                rmax = r_new if rmax is None else jnp.maximum(rmax, r_new)
        for u, (g, p) in enumerate(units):
            cs = slice(p * LANES, (p + 1) * LANES)
            vb = v_ref[g, pl.ds(off, SB_TK), cs]
            v2 = jnp.concatenate([vb * lo_m, vb * hi_m], axis=0)
            acc[u] += _dot(w_scr[base + u], v2)
        return jnp.max(rmax)

    def walk_rest(j_start, rmax_start):
        def cond(carry):
            j, rmax = carry
            return jnp.logical_and(j >= 0, rmax > SB_SKIP_BELOW * LOG2_E)

        def body(carry):
            j, _ = carry
            block_front(j, False, 0)
            return j - 1, block_back(j, 0)

        lax.while_loop(cond, body, (j_start, rmax_start))

    n_ahead = SB_STATIC_BLOCKS - 1

    @pl.when(qi >= n_ahead)
    def _():
        for b in range(SB_STATIC_BLOCKS):
            block_front(qi - b, b == 0, b)
        for b in range(SB_STATIC_BLOCKS):
            rmax = block_back(qi - b, b)
        walk_rest(qi - SB_STATIC_BLOCKS, rmax)

    @pl.when(qi < n_ahead)
    def _():
        block_front(qi, True, 0)
        walk_rest(qi - 1, block_back(qi, 0))

    for g in range(n_group):
        o = jnp.concatenate([acc[g * n_pairs + p] for p in range(n_pairs)], axis=1)
        o_ref[g] = _rms(o, g_ref[...]).astype(BF16)


def _sb_attention(qkv, u2, g, batch):
    t, n3 = qkv.shape
    n = n3 // 3
    s = t // batch
    nq = s // SB_TQ
    n_pairs = n // LANES
    group = SB_GROUP if batch % SB_GROUP == 0 else 1
    n_units = group * n_pairs
    qkv3 = qkv.reshape(batch, s, n3)
    out = pl.pallas_call(
        _sb_kernel,
        grid=(batch // group, nq),
        in_specs=[
            pl.BlockSpec((group, SB_TQ, n), lambda b, i: (b, i, 0)),
            pl.BlockSpec((group, s, n), lambda b, i: (b, 0, 1)),
            pl.BlockSpec((group, s, n), lambda b, i: (b, 0, 2)),
            pl.BlockSpec((4 * SB_TK, 2 * SB_TK), lambda b, i: (0, 0)),
            pl.BlockSpec((1, n), lambda b, i: (0, 0)),
        ],
        out_specs=pl.BlockSpec((group, SB_TQ, n), lambda b, i: (b, i, 0)),
        out_shape=jax.ShapeDtypeStruct((batch, s, n), BF16),
        scratch_shapes=[
            pltpu.VMEM((n_units, SB_TQ, LANES), F32),
            pltpu.VMEM((2 * n_units, SB_TQ, LANES), F32),
            pltpu.VMEM((SB_STATIC_BLOCKS * n_units, SB_TQ, 2 * SB_TK), F32),
            pltpu.VMEM((SB_STATIC_BLOCKS * n_units, SB_TQ, 4 * SB_TK), BF16),
            pltpu.VMEM((SB_STATIC_BLOCKS * n_units, SB_TQ, 2 * SB_TK), F32),
            pltpu.VMEM((SB_STATIC_BLOCKS * n_units, SB_TQ, 2 * SB_TK), BF16),
        ],
        compiler_params=_cparams(("parallel", "arbitrary")),
        name="sb_attn",
    )(qkv3, qkv3, qkv3, u2, g)
    return out.reshape(t, n)


def _memkv_kernel(m_ref, g_ref, wk_ref, wv_ref, k_ref, v_ref):
    mn = _rms(m_ref[...], g_ref[...]).astype(BF16)
    k_ref[...] = _dot(mn, wk_ref[...]).astype(BF16)
    v_ref[...] = _dot(mn, wv_ref[...]).astype(BF16)


def _mem_kv(mem2d, g, wk, wv, layer, batch):
    tmem, d = mem2d.shape
    m = tmem // batch
    n = wk.shape[2]
    const = lambda b: (0, 0)
    row = lambda b: (b, 0)
    return pl.pallas_call(
        _memkv_kernel,
        grid=(batch,),
        in_specs=[
            pl.BlockSpec((m, d), row),
            pl.BlockSpec((1, d), const),
            pl.BlockSpec((None, d, n), lambda b: (layer, 0, 0)),
            pl.BlockSpec((None, d, n), lambda b: (layer, 0, 0)),
        ],
        out_specs=[pl.BlockSpec((m, n), row), pl.BlockSpec((m, n), row)],
        out_shape=[jax.ShapeDtypeStruct((tmem, n), BF16)] * 2,
        compiler_params=_cparams(("parallel",)),
        name="mem_kv",
    )(mem2d, g, wk, wv)


def _tail_kernel(h_ref, ya_ref, yb_ref, wa_ref, wb_ref, gx_ref, wq_ref, k_ref, v_ref, wo_ref, gf_ref, w1_ref, w2_ref,
                 gfin_ref, o_ref, *, ff_chunk, final_norm):
    h = h_ref[...] + _dot(ya_ref[...], wa_ref[...]) + _dot(yb_ref[...], wb_ref[...])

    hn = _rms(h, gx_ref[...]).astype(BF16)
    q = _dot(hn, wq_ref[...]).astype(BF16)
    scale = 1.0 / math.sqrt(XA_HEAD_DIM)
    outs = []
    for hd in range(q.shape[1] // XA_HEAD_DIM):
        cs = slice(hd * XA_HEAD_DIM, (hd + 1) * XA_HEAD_DIM)
        logits = _dot_nt(q[:, cs], k_ref[:, cs]) * scale
        e = jnp.exp(logits - jnp.max(logits, axis=-1, keepdims=True))
        denom = jnp.sum(e, axis=-1, keepdims=True)
        outs.append(_dot(e.astype(BF16), v_ref[:, cs]) / denom)
    h = h + _dot(jnp.concatenate(outs, axis=1).astype(BF16), wo_ref[...])

    hn = _rms(h, gf_ref[...]).astype(BF16)
    out = h
    for c in range(w1_ref.shape[1] // ff_chunk):
        cs = slice(c * ff_chunk, (c + 1) * ff_chunk)
        u = jnp.maximum(_dot(hn, w1_ref[:, cs]), 0.0)
        out = out + _dot((u * u).astype(BF16), w2_ref[cs, :])
    if final_norm:
        out = _rms(out, gfin_ref[...])
    o_ref[...] = out


def _layer_tail(h, ya, yb, w_out, gx, wq, k, v, wo, gf, w1, w2, gfin, layer, batch, final_norm, tm=512, ff_chunk=1024):
    t, d = h.shape
    na, nb = ya.shape[1], yb.shape[1]
    assert na == nb
    n = wq.shape[2]
    f = w1.shape[2]
    m = k.shape[0] // batch
    tiles_per_batch = t // batch // tm
    row = lambda i: (i, 0)
    kv = lambda i: (i // tiles_per_batch, 0)
    resident = lambda shape: pl.BlockSpec(shape, lambda i: (0, 0), pipeline_mode=pl.Buffered(1))
    weight = lambda shape, r=0: pl.BlockSpec((None,) + shape, lambda i: (layer, r, 0), pipeline_mode=pl.Buffered(1))
    return pl.pallas_call(
        functools.partial(_tail_kernel, ff_chunk=ff_chunk, final_norm=final_norm),
        grid=(t // tm,),
        in_specs=[
            pl.BlockSpec((tm, d), row),
            pl.BlockSpec((tm, na), row),
            pl.BlockSpec((tm, nb), row),
            weight((na, d), 0),
            weight((nb, d), 1),
            resident((1, d)),
            weight((d, n)),
            pl.BlockSpec((m, n), kv),
            pl.BlockSpec((m, n), kv),
            weight((n, d)),
            resident((1, d)),
            weight((d, f)),
            weight((f, d)),
            resident((1, d)),
        ],
        out_specs=pl.BlockSpec((tm, d), row),
        out_shape=jax.ShapeDtypeStruct((t, d), F32),
        compiler_params=_cparams(("parallel",)),
        name="layer_tail",
    )(h, ya, yb, w_out, w_out, gx, wq, k, v, wo, gf, w1, w2, gfin)


def _pad_lanes(v):
    return jnp.pad(v.astype(F32), (0, LANES - v.shape[0])).reshape(1, LANES)


def kernel(x, mem, norm_mix_g, w_in, conv_w, conv_b, dt_bias, a_log, d_skip, ssd_norm_g, sb_norm_g, w_out,
           norm_xa_g, norm_mem_g, w_xq, w_xk, w_xv, w_xo, norm_ff_g, w_ff1, w_ff2, final_g):
    batch, seq, d = x.shape
    depth = w_in.shape[0]
    n_heads = dt_bias.shape[1]
    t = batch * seq
    row = lambda v: v.astype(F32).reshape(1, -1)

    jj = lax.broadcasted_iota(jnp.int32, (2 * SB_TK, 2 * SB_TK), 0)
    ss = lax.broadcasted_iota(jnp.int32, (2 * SB_TK, 2 * SB_TK), 1)
    u2 = -((jj // SB_TK == ss // SB_TK) & (jj >= ss)).astype(BF16)
    u2 = jnp.concatenate([u2, u2], axis=0)

    w_out_b, w_xq_b, w_xk_b, w_xv_b, w_xo_b, w_ff1_b, w_ff2_b = (
        w.astype(BF16) for w in (w_out, w_xq, w_xk, w_xv, w_xo, w_ff1, w_ff2))
    w_in = w_in.astype(F32)

    h = x.reshape(t, d)
    mem2d = mem.reshape(batch * mem.shape[1], d)
    for l in range(depth):
        qkv, y_ssd = _proj_ssd(h, row(norm_mix_g[l]), w_in, l, conv_w[l].astype(F32)[:, None, :],
                               row(conv_b[l]), _pad_lanes(dt_bias[l]), _pad_lanes(a_log[l]),
                               row(jnp.repeat(d_skip[l], SSD_HEAD_DIM)), row(ssd_norm_g[l]), batch, n_heads)
        y_sb = _sb_attention(qkv, u2, row(sb_norm_g[l]), batch)
        k_mem, v_mem = _mem_kv(mem2d, row(norm_mem_g[l]), w_xk_b, w_xv_b, l, batch)
        h = _layer_tail(h, y_ssd, y_sb, w_out_b, row(norm_xa_g[l]), w_xq_b, k_mem, v_mem, w_xo_b,
                        row(norm_ff_g[l]), w_ff1_b, w_ff2_b, row(final_g), l, batch, final_norm=(l == depth - 1))
    return h.reshape(batch, seq, d)
```

```python
import functools
import math

import jax
import jax.numpy as jnp
from jax import lax
from jax.experimental import pallas as pl
from jax.experimental.pallas import tpu as pltpu

F32 = jnp.float32
BF16 = jnp.bfloat16

EPS = 1e-5
CONV_K = 4
CHUNK = 128
SSD_HEAD_DIM = 64
SSD_GROUPS = 2
SSD_STATE = 64
SB_HEAD_DIM = 64
XA_HEAD_DIM = 128

LANES = 128
SUBLANES = 8
MXU_N = 256
VMEM_LIMIT = 56 * 1024 * 1024

SB_TQ = 128
SB_TK = 128
SB_GROUP = 2
SB_STATIC_BLOCKS = 3
SB_SKIP_BELOW = -110.0
NEG_BIG = -1e30
LOG2_E = math.log2(math.e)


def _cparams(sem):
    return pltpu.CompilerParams(dimension_semantics=sem, vmem_limit_bytes=VMEM_LIMIT)


def _rms(x, g):
    return x * lax.rsqrt(jnp.mean(x * x, axis=-1, keepdims=True) + EPS) * g


def _softplus(x):
    return jnp.maximum(x, 0.0) + jnp.log(1.0 + jnp.exp(-jnp.abs(x)))


def _dot(a, b):
    return jnp.dot(a, b, preferred_element_type=F32)


def _dot_nt(a, b):
    return lax.dot_general(a, b, (((1,), (1,)), ((), ())), preferred_element_type=F32)


def _split_bf16(x, parts):
    out = []
    r = x
    for _ in range(parts - 1):
        p = r.astype(BF16)
        out.append(p)
        r = r - p.astype(F32)
    out.append(r.astype(BF16))
    return out


def _expand_heads(a, lane):
    cols = []
    for j in range(4):
        lo = jnp.broadcast_to(a[:, 2 * j:2 * j + 1], (a.shape[0], LANES))
        hi = jnp.broadcast_to(a[:, 2 * j + 1:2 * j + 2], (a.shape[0], LANES))
        cols.append(jnp.where(lane < SSD_HEAD_DIM, lo, hi))
    return jnp.concatenate(cols, axis=1)


def _ssd_chunk(z, window, dt_raw, st, cw_ref, cb_ref, dtb_ref, alog_ref, dsk_ref, gn_ref, interleave):
    n_inner = z.shape[1]
    halo = window.shape[0] - CHUNK
    interleave = list(interleave)
    per_stage = -(-len(interleave) // 3)

    def run_interleaved(stage):
        for job in interleave[stage * per_stage:(stage + 1) * per_stage]:
            job()

    run_interleaved(0)

    conv = cb_ref[...] + cw_ref[CONV_K - 1] * window[halo:, :]
    for k in range(CONV_K - 1):
        shifted = pltpu.roll(window, CONV_K - 1 - k, 0)[halo:, :]
        conv = conv + cw_ref[k] * shifted
    xbc = conv * jax.nn.sigmoid(conv)
    xs = xbc[:, :n_inner]
    bm = xbc[:, n_inner:n_inner + LANES]
    cm = xbc[:, n_inner + LANES:]

    row = lax.broadcasted_iota(jnp.int32, (CHUNK, CHUNK), 0)
    col = lax.broadcasted_iota(jnp.int32, (CHUNK, CHUNK), 1)
    causal = row >= col
    lane = col

    dt = _softplus(dt_raw + dtb_ref[...])
    a_c = dt * (-jnp.exp(alog_ref[...]))
    tri = jnp.where(causal, 1.0, 0.0).astype(BF16)
    a_cum = None
    for part in _split_bf16(a_c, 3):
        term = _dot(tri, part)
        a_cum = term if a_cum is None else a_cum + term
    a_cum_t = a_cum.T

    dt_x = _expand_heads(dt, lane)
    acum_x = _expand_heads(a_cum, lane)
    last_x = acum_x[CHUNK - 1:CHUNK, :]
    xc = xs * dt_x
    e_acum = jnp.exp(acum_x)
    xd = (xc * jnp.exp(last_x - acum_x)).astype(BF16)
    chunk_decay = jnp.exp(last_x)
    run_interleaved(1)

    bm_b = bm.astype(BF16)
    cbs = []
    for g in range(SSD_GROUPS):
        in_g = (lane >= g * SSD_STATE) & (lane < (g + 1) * SSD_STATE)
        cbs.append(_dot_nt(jnp.where(in_g, cm, 0.0).astype(BF16), bm_b))
    y_cols = []
    for j in range(n_inner // LANES):
        ms = []
        for h in (2 * j, 2 * j + 1):
            seg = a_cum[:, h:h + 1] - a_cum_t[h:h + 1, :]
            decay = jnp.exp(jnp.where(causal, seg, NEG_BIG))
            ms.append((cbs[h * SSD_HEAD_DIM * SSD_GROUPS // n_inner] * decay).astype(BF16))
        xcj = xc[:, j * LANES:(j + 1) * LANES]
        x_lo = jnp.where(lane < SSD_HEAD_DIM, xcj, 0.0).astype(BF16)
        x_hi = jnp.where(lane >= SSD_HEAD_DIM, xcj, 0.0).astype(BF16)
        y_cols.append(_dot(jnp.concatenate(ms, axis=1), jnp.concatenate([x_lo, x_hi], axis=0)))
    y = jnp.concatenate(y_cols, axis=1)
    run_interleaved(2)

    y = y + _dot(cm.astype(BF16), st.astype(BF16)) * e_acum
    srow = lax.broadcasted_iota(jnp.int32, st.shape, 0)
    scol = lax.broadcasted_iota(jnp.int32, st.shape, 1)
    same_group = (srow // SSD_STATE) == (scol // (n_inner // SSD_GROUPS))
    new_st = st * chunk_decay + jnp.where(same_group, _dot(bm.T.astype(BF16), xd), 0.0)

    y = y + dsk_ref[...] * xs
    y = y * (z * jax.nn.sigmoid(z))
    return _rms(y, gn_ref[...]).astype(BF16), new_st


def _proj_ssd_kernel(h_ref, g_ref, win_ref, cw_ref, cb_ref, dtb_ref, alog_ref, dsk_ref, gn_ref,
                     qkv_ref, y_ref, wzx_ref, wdt_ref, wqkv_ref, zx_a, dt_a, zx_b, dt_b, halo_scr, state,
                     *, tiles_per_batch, n_heads):
    s = pl.program_id(0)
    tm = h_ref.shape[0]
    n_inner = y_ref.shape[1]

    @pl.when(s == 0)
    def _():
        zx_b[...] = jnp.zeros(zx_b.shape, F32)
        dt_b[...] = jnp.zeros(dt_b.shape, F32)
        halo_scr[...] = jnp.zeros(halo_scr.shape, F32)
        state[...] = jnp.zeros(state.shape, F32)
        nzx, nqkv = wzx_ref.shape[0], wqkv_ref.shape[0]
        n_sb = nqkv // 3
        q_scale = 1.0 / math.sqrt(SB_HEAD_DIM)
        for r in range(0, nzx, CHUNK):
            wzx_ref[r:r + CHUNK, :] = win_ref[r:r + CHUNK, :].astype(BF16)
        row = lax.broadcasted_iota(jnp.int32, wdt_ref.shape, 0)
        wdt_ref[...] = jnp.where(row < n_heads, win_ref[nzx:nzx + wdt_ref.shape[0], :], 0.0).astype(BF16)
        for r in range(0, nqkv, CHUNK):
            blk = win_ref[nzx + n_heads + r:nzx + n_heads + r + CHUNK, :]
            wqkv_ref[r:r + CHUNK, :] = (blk * q_scale if r < n_sb else blk).astype(BF16)

    def step(zx_w, dt_w, zx_r, dt_r):
        hn = _rms(h_ref[...], g_ref[...]).astype(BF16)
        n_chunks = tm // CHUNK
        nzx, nqkv = zx_w.shape[1], qkv_ref.shape[1]

        def project(out_ref, w_ref, a):
            b = min(a + MXU_N, w_ref.shape[0])
            out_ref[:, a:b] = _dot_nt(hn, w_ref[a:b, :]).astype(out_ref.dtype)

        jobs = [functools.partial(project, zx_w, wzx_ref, a) for a in range(0, nzx, MXU_N)]
        jobs += [functools.partial(project, dt_w, wdt_ref, 0)]
        jobs += [functools.partial(project, qkv_ref, wqkv_ref, a) for a in range(0, nqkv, MXU_N)]
        share = -(-len(jobs) // n_chunks)

        fresh = lax.rem(s - 1, tiles_per_batch) == 0
        st = jnp.where(fresh, 0.0, state[...])
        halo = jnp.where(fresh, 0.0, halo_scr[...])
        for c in range(n_chunks):
            r0 = c * CHUNK
            if c == 0:
                window = jnp.concatenate([halo, zx_r[0:CHUNK, n_inner:]], axis=0)
            else:
                window = zx_r[r0 - SUBLANES:r0 + CHUNK, n_inner:]
            y, st = _ssd_chunk(zx_r[r0:r0 + CHUNK, :n_inner], window, dt_r[r0:r0 + CHUNK, :], st,
                               cw_ref, cb_ref, dtb_ref, alog_ref, dsk_ref, gn_ref, jobs[c * share:(c + 1) * share])
            y_ref[r0:r0 + CHUNK, :] = y
        state[...] = st
        halo_scr[...] = zx_r[tm - SUBLANES:tm, n_inner:]

    @pl.when(s % 2 == 0)
    def _():
        step(zx_a, dt_a, zx_b, dt_b)

    @pl.when(s % 2 == 1)
    def _():
        step(zx_b, dt_b, zx_a, dt_a)


def _proj_ssd(h, g, w_in, layer, cw, cb, dtb, alog, dsk, gn, batch, n_heads, tm=512):
    t, d = h.shape
    n_inner = gn.shape[1]
    n_conv = cb.shape[1]
    nzx, ndt = n_inner + n_conv, LANES
    in_dim = w_in.shape[1]
    nqkv = in_dim - nzx - n_heads
    n_tiles = t // tm
    last = n_tiles - 1
    resident = lambda shape: pl.BlockSpec(shape, lambda s: (0,) * len(shape), pipeline_mode=pl.Buffered(1))
    return pl.pallas_call(
        functools.partial(_proj_ssd_kernel, tiles_per_batch=n_tiles // batch, n_heads=n_heads),
        grid=(n_tiles + 1,),
        in_specs=[
            pl.BlockSpec((tm, d), lambda s: (jnp.minimum(s, last), 0)),
            resident((1, d)),
            pl.BlockSpec((None, in_dim, d), lambda s: (layer, 0, 0), pipeline_mode=pl.Buffered(1)),
            resident((CONV_K, 1, n_conv)),
            resident((1, n_conv)),
            resident((1, LANES)),
            resident((1, LANES)),
            resident((1, n_inner)),
            resident((1, n_inner)),
        ],
        out_specs=[
            pl.BlockSpec((tm, nqkv), lambda s: (jnp.minimum(s, last), 0)),
            pl.BlockSpec((tm, n_inner), lambda s: (jnp.maximum(s - 1, 0), 0)),
        ],
        out_shape=[
            jax.ShapeDtypeStruct((t, nqkv), BF16),
            jax.ShapeDtypeStruct((t, n_inner), BF16),
        ],
        scratch_shapes=[
            pltpu.VMEM((nzx, d), BF16),
            pltpu.VMEM((ndt, d), BF16),
            pltpu.VMEM((nqkv, d), BF16),
            pltpu.VMEM((tm, nzx), F32),
            pltpu.VMEM((tm, ndt), F32),
            pltpu.VMEM((tm, nzx), F32),
            pltpu.VMEM((tm, ndt), F32),
            pltpu.VMEM((SUBLANES, n_conv), F32),
            pltpu.VMEM((SSD_GROUPS * SSD_STATE, n_inner), F32),
        ],
        compiler_params=_cparams(("arbitrary",)),
        name="proj_ssd",
    )(h, g, w_in, cw, cb, dtb, alog, dsk, gn)


def _sb_kernel(q_ref, k_ref, v_ref, u2_ref, g_ref, o_ref, acc, rsum, z_scr, hl_scr, s_scr, w_scr):
    n_group = q_ref.shape[0]
    n_pairs = q_ref.shape[2] // LANES
    units = [(g, p) for g in range(n_group) for p in range(n_pairs)]
    qi = pl.program_id(1)
    acc[...] = jnp.zeros(acc.shape, F32)
    rsum[...] = jnp.zeros(rsum.shape, F32)

    row2 = lax.broadcasted_iota(jnp.int32, (SB_TQ, 2 * SB_TK), 0)
    col2 = lax.broadcasted_iota(jnp.int32, (SB_TQ, 2 * SB_TK), 1)
    below = (col2 & (SB_TK - 1)) < row2
    lane = lax.broadcasted_iota(jnp.int32, (SB_TK, LANES), 1)
    lo_m = jnp.where(lane < SB_HEAD_DIM, 1.0, 0.0).astype(BF16)
    hi_m = jnp.where(lane >= SB_HEAD_DIM, 1.0, 0.0).astype(BF16)

    n_units = len(units)

    def block_front(j, diagonal, slot):
        off = pl.multiple_of(j * SB_TK, SB_TK)
        base = slot * n_units
        for u, (g, p) in enumerate(units):
            cs = slice(p * LANES, (p + 1) * LANES)
            kb = k_ref[g, pl.ds(off, SB_TK), cs]
            k2 = jnp.concatenate([kb * lo_m, kb * hi_m], axis=0)
            z = _dot_nt(q_ref[g, :, cs], k2)
            z_scr[base + u] = z * LOG2_E
        for u in range(n_units):
            z2 = z_scr[base + u]
            sp2 = jnp.maximum(z2, 0.0) + jnp.log2(1.0 + jnp.exp2(-jnp.abs(z2)))
            if diagonal:
                sp2 = jnp.where(below, sp2, 0.0)
                z_scr[base + u] = jnp.where(below, z2, NEG_BIG)
            hi, lo = _split_bf16(sp2, 2)
            hl_scr[base + u, :, 0:2 * SB_TK] = hi
            hl_scr[base + u, :, 2 * SB_TK:4 * SB_TK] = lo
        for u in range(n_units):
            s_scr[base + u] = _dot(hl_scr[base + u], u2_ref[...])

    def block_back(j, slot):
        off = pl.multiple_of(j * SB_TK, SB_TK)
        base = slot * n_units
        rmax = None
        for u in range(n_units):
            r_prev = jnp.concatenate([rsum[2 * u], rsum[2 * u + 1]], axis=1)
            s = s_scr[base + u] + r_prev
            w_scr[base + u] = jnp.exp2(z_scr[base + u] + s).astype(BF16)
            for e in range(2):
                r_new = jnp.broadcast_to(s[:, e * SB_TK:e * SB_TK + 1], (SB_TQ, LANES))
                rsum[2 * u + e] = r_new
                rmax = r_new if rmax is None else jnp.maximum(rmax, r_new)
        for u, (g, p) in enumerate(units):
            cs = slice(p * LANES, (p + 1) * LANES)
            vb = v_ref[g, pl.ds(off, SB_TK), cs]
            v2 = jnp.concatenate([vb * lo_m, vb * hi_m], axis=0)
            acc[u] += _dot(w_scr[base + u], v2)
        return jnp.max(rmax)

    def walk_rest(j_start, rmax_start):
        def cond(carry):
            j, rmax = carry
            return jnp.logical_and(j >= 0, rmax > SB_SKIP_BELOW * LOG2_E)

        def body(carry):
            j, _ = carry
            block_front(j, False, 0)
            return j - 1, block_back(j, 0)

        lax.while_loop(cond, body, (j_start, rmax_start))

    n_ahead = SB_STATIC_BLOCKS - 1

    @pl.when(qi >= n_ahead)
    def _():
        for b in range(SB_STATIC_BLOCKS):
            block_front(qi - b, b == 0, b)
        for b in range(SB_STATIC_BLOCKS):
            rmax = block_back(qi - b, b)
        walk_rest(qi - SB_STATIC_BLOCKS, rmax)

    @pl.when(qi < n_ahead)
    def _():
        block_front(qi, True, 0)
        walk_rest(qi - 1, block_back(qi, 0))

    for g in range(n_group):
        o = jnp.concatenate([acc[g * n_pairs + p] for p in range(n_pairs)], axis=1)
        o_ref[g] = _rms(o, g_ref[...]).astype(BF16)


def _sb_attention(qkv, u2, g, batch):
    t, n3 = qkv.shape
    n = n3 // 3
    s = t // batch
    nq = s // SB_TQ
    n_pairs = n // LANES
    group = SB_GROUP if batch % SB_GROUP == 0 else 1
    n_units = group * n_pairs
    qkv3 = qkv.reshape(batch, s, n3)
    out = pl.pallas_call(
        _sb_kernel,
        grid=(batch // group, nq),
        in_specs=[
            pl.BlockSpec((group, SB_TQ, n), lambda b, i: (b, i, 0)),
            pl.BlockSpec((group, s, n), lambda b, i: (b, 0, 1)),
            pl.BlockSpec((group, s, n), lambda b, i: (b, 0, 2)),
            pl.BlockSpec((4 * SB_TK, 2 * SB_TK), lambda b, i: (0, 0)),
            pl.BlockSpec((1, n), lambda b, i: (0, 0)),
        ],
        out_specs=pl.BlockSpec((group, SB_TQ, n), lambda b, i: (b, i, 0)),
        out_shape=jax.ShapeDtypeStruct((batch, s, n), BF16),
        scratch_shapes=[
            pltpu.VMEM((n_units, SB_TQ, LANES), F32),
            pltpu.VMEM((2 * n_units, SB_TQ, LANES), F32),
            pltpu.VMEM((SB_STATIC_BLOCKS * n_units, SB_TQ, 2 * SB_TK), F32),
            pltpu.VMEM((SB_STATIC_BLOCKS * n_units, SB_TQ, 4 * SB_TK), BF16),
            pltpu.VMEM((SB_STATIC_BLOCKS * n_units, SB_TQ, 2 * SB_TK), F32),
            pltpu.VMEM((SB_STATIC_BLOCKS * n_units, SB_TQ, 2 * SB_TK), BF16),
        ],
        compiler_params=_cparams(("parallel", "arbitrary")),
        name="sb_attn",
    )(qkv3, qkv3, qkv3, u2, g)
    return out.reshape(t, n)


def _memkv_kernel(m_ref, g_ref, wk_ref, wv_ref, k_ref, v_ref):
    mn = _rms(m_ref[...], g_ref[...]).astype(BF16)
    k_ref[...] = _dot(mn, wk_ref[...]).astype(BF16)
    v_ref[...] = _dot(mn, wv_ref[...]).astype(BF16)


def _mem_kv(mem2d, g, wk, wv, layer, batch):
    tmem, d = mem2d.shape
    m = tmem // batch
    n = wk.shape[2]
    const = lambda b: (0, 0)
    row = lambda b: (b, 0)
    return pl.pallas_call(
        _memkv_kernel,
        grid=(batch,),
        in_specs=[
            pl.BlockSpec((m, d), row),
            pl.BlockSpec((1, d), const),
            pl.BlockSpec((None, d, n), lambda b: (layer, 0, 0)),
            pl.BlockSpec((None, d, n), lambda b: (layer, 0, 0)),
        ],
        out_specs=[pl.BlockSpec((m, n), row), pl.BlockSpec((m, n), row)],
        out_shape=[jax.ShapeDtypeStruct((tmem, n), BF16)] * 2,
        compiler_params=_cparams(("parallel",)),
        name="mem_kv",
    )(mem2d, g, wk, wv)


def _tail_kernel(h_ref, ya_ref, yb_ref, wa_ref, wb_ref, gx_ref, wq_ref, k_ref, v_ref, wo_ref, gf_ref, w1_ref, w2_ref,
                 gfin_ref, o_ref, *, ff_chunk, final_norm):
    h = h_ref[...] + _dot(ya_ref[...], wa_ref[...]) + _dot(yb_ref[...], wb_ref[...])

    hn = _rms(h, gx_ref[...]).astype(BF16)
    q = _dot(hn, wq_ref[...]).astype(BF16)
    scale = 1.0 / math.sqrt(XA_HEAD_DIM)
    outs = []
    for hd in range(q.shape[1] // XA_HEAD_DIM):
        cs = slice(hd * XA_HEAD_DIM, (hd + 1) * XA_HEAD_DIM)
        logits = _dot_nt(q[:, cs], k_ref[:, cs]) * scale
        e = jnp.exp(logits - jnp.max(logits, axis=-1, keepdims=True))
        denom = jnp.sum(e, axis=-1, keepdims=True)
        outs.append(_dot(e.astype(BF16), v_ref[:, cs]) / denom)
    h = h + _dot(jnp.concatenate(outs, axis=1).astype(BF16), wo_ref[...])

    hn = _rms(h, gf_ref[...]).astype(BF16)
    out = h
    for c in range(w1_ref.shape[1] // ff_chunk):
        cs = slice(c * ff_chunk, (c + 1) * ff_chunk)
        u = jnp.maximum(_dot(hn, w1_ref[:, cs]), 0.0)
        out = out + _dot((u * u).astype(BF16), w2_ref[cs, :])
    if final_norm:
        out = _rms(out, gfin_ref[...])
    o_ref[...] = out


def _layer_tail(h, ya, yb, w_out, gx, wq, k, v, wo, gf, w1, w2, gfin, layer, batch, final_norm, tm=512, ff_chunk=1024):
    t, d = h.shape
    na, nb = ya.shape[1], yb.shape[1]
    assert na == nb
    n = wq.shape[2]
    f = w1.shape[2]
    m = k.shape[0] // batch
    tiles_per_batch = t // batch // tm
    row = lambda i: (i, 0)
    kv = lambda i: (i // tiles_per_batch, 0)
    resident = lambda shape: pl.BlockSpec(shape, lambda i: (0, 0), pipeline_mode=pl.Buffered(1))
    weight = lambda shape, r=0: pl.BlockSpec((None,) + shape, lambda i: (layer, r, 0), pipeline_mode=pl.Buffered(1))
    return pl.pallas_call(
        functools.partial(_tail_kernel, ff_chunk=ff_chunk, final_norm=final_norm),
        grid=(t // tm,),
        in_specs=[
            pl.BlockSpec((tm, d), row),
            pl.BlockSpec((tm, na), row),
            pl.BlockSpec((tm, nb), row),
            weight((na, d), 0),
            weight((nb, d), 1),
            resident((1, d)),
            weight((d, n)),
            pl.BlockSpec((m, n), kv),
            pl.BlockSpec((m, n), kv),
            weight((n, d)),
            resident((1, d)),
            weight((d, f)),
            weight((f, d)),
            resident((1, d)),
        ],
        out_specs=pl.BlockSpec((tm, d), row),
        out_shape=jax.ShapeDtypeStruct((t, d), F32),
        compiler_params=_cparams(("parallel",)),
        name="layer_tail",
    )(h, ya, yb, w_out, w_out, gx, wq, k, v, wo, gf, w1, w2, gfin)


def _pad_lanes(v):
    return jnp.pad(v.astype(F32), (0, LANES - v.shape[0])).reshape(1, LANES)


def kernel(x, mem, norm_mix_g, w_in, conv_w, conv_b, dt_bias, a_log, d_skip, ssd_norm_g, sb_norm_g, w_out,
           norm_xa_g, norm_mem_g, w_xq, w_xk, w_xv, w_xo, norm_ff_g, w_ff1, w_ff2, final_g):
    batch, seq, d = x.shape
    depth = w_in.shape[0]
    n_heads = dt_bias.shape[1]
    t = batch * seq
    row = lambda v: v.astype(F32).reshape(1, -1)

    jj = lax.broadcasted_iota(jnp.int32, (2 * SB_TK, 2 * SB_TK), 0)
    ss = lax.broadcasted_iota(jnp.int32, (2 * SB_TK, 2 * SB_TK), 1)
    u2 = -((jj // SB_TK == ss // SB_TK) & (jj >= ss)).astype(BF16)
    u2 = jnp.concatenate([u2, u2], axis=0)

    w_out_b, w_xq_b, w_xk_b, w_xv_b, w_xo_b, w_ff1_b, w_ff2_b = (
        w.astype(BF16) for w in (w_out, w_xq, w_xk, w_xv, w_xo, w_ff1, w_ff2))
    w_in = jnp.swapaxes(w_in.astype(F32), 1, 2)

    h = x.reshape(t, d)
    mem2d = mem.reshape(batch * mem.shape[1], d)
    for l in range(depth):
        qkv, y_ssd = _proj_ssd(h, row(norm_mix_g[l]), w_in, l, conv_w[l].astype(F32)[:, None, :],
                               row(conv_b[l]), _pad_lanes(dt_bias[l]), _pad_lanes(a_log[l]),
                               row(jnp.repeat(d_skip[l], SSD_HEAD_DIM)), row(ssd_norm_g[l]), batch, n_heads)
        y_sb = _sb_attention(qkv, u2, row(sb_norm_g[l]), batch)
        k_mem, v_mem = _mem_kv(mem2d, row(norm_mem_g[l]), w_xk_b, w_xv_b, l, batch)
        h = _layer_tail(h, y_ssd, y_sb, w_out_b, row(norm_xa_g[l]), w_xq_b, k_mem, v_mem, w_xo_b,
                        row(norm_ff_g[l]), w_ff1_b, w_ff2_b, row(final_g), l, batch, final_norm=(l == depth - 1))
    return h.reshape(batch, seq, d)
```

```python
import functools
import math

import jax
import jax.numpy as jnp
from jax import lax
from jax.experimental import pallas as pl
from jax.experimental.pallas import tpu as pltpu

F32 = jnp.float32
BF16 = jnp.bfloat16

EPS = 1e-5
CONV_K = 4
CHUNK = 128
SSD_HEAD_DIM = 64
SSD_GROUPS = 2
SSD_STATE = 64
SB_HEAD_DIM = 64
XA_HEAD_DIM = 128

LANES = 128
SUBLANES = 8
MXU_N = 256
VMEM_LIMIT = 56 * 1024 * 1024

SB_TQ = 128
SB_TK = 128
SB_GROUP = 2
SB_STATIC_BLOCKS = 3
SB_QB_PER_STEP = 4
SB_SKIP_BELOW = -110.0
NEG_BIG = -1e30
LOG2_E = math.log2(math.e)


def _cparams(sem):
    return pltpu.CompilerParams(dimension_semantics=sem, vmem_limit_bytes=VMEM_LIMIT)


def _rms(x, g):
    return x * lax.rsqrt(jnp.mean(x * x, axis=-1, keepdims=True) + EPS) * g


def _softplus(x):
    return jnp.maximum(x, 0.0) + jnp.log(1.0 + jnp.exp(-jnp.abs(x)))


def _dot(a, b):
    return jnp.dot(a, b, preferred_element_type=F32)


def _dot_nt(a, b):
    return lax.dot_general(a, b, (((1,), (1,)), ((), ())), preferred_element_type=F32)


def _split_bf16(x, parts):
    out = []
    r = x
    for _ in range(parts - 1):
        p = r.astype(BF16)
        out.append(p)
        r = r - p.astype(F32)
    out.append(r.astype(BF16))
    return out


def _expand_heads(a, lane):
    cols = []
    for j in range(4):
        lo = jnp.broadcast_to(a[:, 2 * j:2 * j + 1], (a.shape[0], LANES))
        hi = jnp.broadcast_to(a[:, 2 * j + 1:2 * j + 2], (a.shape[0], LANES))
        cols.append(jnp.where(lane < SSD_HEAD_DIM, lo, hi))
    return jnp.concatenate(cols, axis=1)


def _ssd_chunk(z, window, dt_raw, st, cw_ref, cb_ref, dtb_ref, alog_ref, dsk_ref, gn_ref, interleave):
    n_inner = z.shape[1]
    halo = window.shape[0] - CHUNK
    interleave = list(interleave)
    per_stage = -(-len(interleave) // 3)

    def run_interleaved(stage):
        for job in interleave[stage * per_stage:(stage + 1) * per_stage]:
            job()

    run_interleaved(0)

    conv = cb_ref[...] + cw_ref[CONV_K - 1] * window[halo:, :]
    for k in range(CONV_K - 1):
        shifted = pltpu.roll(window, CONV_K - 1 - k, 0)[halo:, :]
        conv = conv + cw_ref[k] * shifted
    xbc = conv * jax.nn.sigmoid(conv)
    xs = xbc[:, :n_inner]
    bm = xbc[:, n_inner:n_inner + LANES]
    cm = xbc[:, n_inner + LANES:]

    row = lax.broadcasted_iota(jnp.int32, (CHUNK, CHUNK), 0)
    col = lax.broadcasted_iota(jnp.int32, (CHUNK, CHUNK), 1)
    causal = row >= col
    lane = col

    dt = _softplus(dt_raw + dtb_ref[...])
    a_c = dt * (-jnp.exp(alog_ref[...]))
    tri = jnp.where(causal, 1.0, 0.0).astype(BF16)
    a_cum = None
    for part in _split_bf16(a_c, 3):
        term = _dot(tri, part)
        a_cum = term if a_cum is None else a_cum + term
    a_cum_t = a_cum.T

    dt_x = _expand_heads(dt, lane)
    acum_x = _expand_heads(a_cum, lane)
    last_x = acum_x[CHUNK - 1:CHUNK, :]
    xc = xs * dt_x
    e_acum = jnp.exp(acum_x)
    xd = (xc * jnp.exp(last_x - acum_x)).astype(BF16)
    chunk_decay = jnp.exp(last_x)
    run_interleaved(1)

    bm_b = bm.astype(BF16)
    cbs = []
    for g in range(SSD_GROUPS):
        in_g = (lane >= g * SSD_STATE) & (lane < (g + 1) * SSD_STATE)
        cbs.append(_dot_nt(jnp.where(in_g, cm, 0.0).astype(BF16), bm_b))
    y_cols = []
    for j in range(n_inner // LANES):
        ms = []
        for h in (2 * j, 2 * j + 1):
            seg = a_cum[:, h:h + 1] - a_cum_t[h:h + 1, :]
            decay = jnp.exp(jnp.where(causal, seg, NEG_BIG))
            ms.append((cbs[h * SSD_HEAD_DIM * SSD_GROUPS // n_inner] * decay).astype(BF16))
        xcj = xc[:, j * LANES:(j + 1) * LANES]
        x_lo = jnp.where(lane < SSD_HEAD_DIM, xcj, 0.0).astype(BF16)
        x_hi = jnp.where(lane >= SSD_HEAD_DIM, xcj, 0.0).astype(BF16)
        y_cols.append(_dot(jnp.concatenate(ms, axis=1), jnp.concatenate([x_lo, x_hi], axis=0)))
    y = jnp.concatenate(y_cols, axis=1)
    run_interleaved(2)

    y = y + _dot(cm.astype(BF16), st.astype(BF16)) * e_acum
    srow = lax.broadcasted_iota(jnp.int32, st.shape, 0)
    scol = lax.broadcasted_iota(jnp.int32, st.shape, 1)
    same_group = (srow // SSD_STATE) == (scol // (n_inner // SSD_GROUPS))
    new_st = st * chunk_decay + jnp.where(same_group, _dot(bm.T.astype(BF16), xd), 0.0)

    y = y + dsk_ref[...] * xs
    y = y * (z * jax.nn.sigmoid(z))
    return _rms(y, gn_ref[...]).astype(BF16), new_st


def _proj_ssd_kernel(h_ref, g_ref, win_ref, cw_ref, cb_ref, dtb_ref, alog_ref, dsk_ref, gn_ref,
                     qkv_ref, y_ref, wzx_ref, wdt_ref, wqkv_ref, zx_a, dt_a, zx_b, dt_b, halo_scr, state,
                     *, tiles_per_batch, n_heads):
    s = pl.program_id(0)
    tm = h_ref.shape[0]
    n_inner = y_ref.shape[1]

    @pl.when(s == 0)
    def _():
        zx_b[...] = jnp.zeros(zx_b.shape, F32)
        dt_b[...] = jnp.zeros(dt_b.shape, F32)
        halo_scr[...] = jnp.zeros(halo_scr.shape, F32)
        state[...] = jnp.zeros(state.shape, F32)
        nzx, nqkv = wzx_ref.shape[0], wqkv_ref.shape[0]
        n_sb = nqkv // 3
        q_scale = 1.0 / math.sqrt(SB_HEAD_DIM)
        for r in range(0, nzx, CHUNK):
            wzx_ref[r:r + CHUNK, :] = win_ref[r:r + CHUNK, :].astype(BF16)
        row = lax.broadcasted_iota(jnp.int32, wdt_ref.shape, 0)
        wdt_ref[...] = jnp.where(row < n_heads, win_ref[nzx:nzx + wdt_ref.shape[0], :], 0.0).astype(BF16)
        for r in range(0, nqkv, CHUNK):
            blk = win_ref[nzx + n_heads + r:nzx + n_heads + r + CHUNK, :]
            wqkv_ref[r:r + CHUNK, :] = (blk * q_scale if r < n_sb else blk).astype(BF16)

    def step(zx_w, dt_w, zx_r, dt_r):
        hn = _rms(h_ref[...], g_ref[...]).astype(BF16)
        n_chunks = tm // CHUNK
        nzx, nqkv = zx_w.shape[1], qkv_ref.shape[1]

        def project(out_ref, w_ref, a):
            b = min(a + MXU_N, w_ref.shape[0])
            out_ref[:, a:b] = _dot_nt(hn, w_ref[a:b, :]).astype(out_ref.dtype)

        jobs = [functools.partial(project, zx_w, wzx_ref, a) for a in range(0, nzx, MXU_N)]
        jobs += [functools.partial(project, dt_w, wdt_ref, 0)]
        jobs += [functools.partial(project, qkv_ref, wqkv_ref, a) for a in range(0, nqkv, MXU_N)]
        share = -(-len(jobs) // n_chunks)

        fresh = lax.rem(s - 1, tiles_per_batch) == 0
        st = jnp.where(fresh, 0.0, state[...])
        halo = jnp.where(fresh, 0.0, halo_scr[...])
        for c in range(n_chunks):
            r0 = c * CHUNK
            if c == 0:
                window = jnp.concatenate([halo, zx_r[0:CHUNK, n_inner:]], axis=0)
            else:
                window = zx_r[r0 - SUBLANES:r0 + CHUNK, n_inner:]
            y, st = _ssd_chunk(zx_r[r0:r0 + CHUNK, :n_inner], window, dt_r[r0:r0 + CHUNK, :], st,
                               cw_ref, cb_ref, dtb_ref, alog_ref, dsk_ref, gn_ref, jobs[c * share:(c + 1) * share])
            y_ref[r0:r0 + CHUNK, :] = y
        state[...] = st
        halo_scr[...] = zx_r[tm - SUBLANES:tm, n_inner:]

    @pl.when(s % 2 == 0)
    def _():
        step(zx_a, dt_a, zx_b, dt_b)

    @pl.when(s % 2 == 1)
    def _():
        step(zx_b, dt_b, zx_a, dt_a)


def _proj_ssd(h, g, w_in, layer, cw, cb, dtb, alog, dsk, gn, batch, n_heads, tm=512):
    t, d = h.shape
    n_inner = gn.shape[1]
    n_conv = cb.shape[1]
    nzx, ndt = n_inner + n_conv, LANES
    in_dim = w_in.shape[1]
    nqkv = in_dim - nzx - n_heads
    n_tiles = t // tm
    last = n_tiles - 1
    resident = lambda shape: pl.BlockSpec(shape, lambda s: (0,) * len(shape), pipeline_mode=pl.Buffered(1))
    return pl.pallas_call(
        functools.partial(_proj_ssd_kernel, tiles_per_batch=n_tiles // batch, n_heads=n_heads),
        grid=(n_tiles + 1,),
        in_specs=[
            pl.BlockSpec((tm, d), lambda s: (jnp.minimum(s, last), 0)),
            resident((1, d)),
            pl.BlockSpec((None, in_dim, d), lambda s: (layer, 0, 0), pipeline_mode=pl.Buffered(1)),
            resident((CONV_K, 1, n_conv)),
            resident((1, n_conv)),
            resident((1, LANES)),
            resident((1, LANES)),
            resident((1, n_inner)),
            resident((1, n_inner)),
        ],
        out_specs=[
            pl.BlockSpec((tm, nqkv), lambda s: (jnp.minimum(s, last), 0)),
            pl.BlockSpec((tm, n_inner), lambda s: (jnp.maximum(s - 1, 0), 0)),
        ],
        out_shape=[
            jax.ShapeDtypeStruct((t, nqkv), BF16),
            jax.ShapeDtypeStruct((t, n_inner), BF16),
        ],
        scratch_shapes=[
            pltpu.VMEM((nzx, d), BF16),
            pltpu.VMEM((ndt, d), BF16),
            pltpu.VMEM((nqkv, d), BF16),
            pltpu.VMEM((tm, nzx), F32),
            pltpu.VMEM((tm, ndt), F32),
            pltpu.VMEM((tm, nzx), F32),
            pltpu.VMEM((tm, ndt), F32),
            pltpu.VMEM((SUBLANES, n_conv), F32),
            pltpu.VMEM((SSD_GROUPS * SSD_STATE, n_inner), F32),
        ],
        compiler_params=_cparams(("arbitrary",)),
        name="proj_ssd",
    )(h, g, w_in, cw, cb, dtb, alog, dsk, gn)


def _sb_kernel(q_ref, k_ref, v_ref, u2_ref, g_ref, o_ref, acc, rsum, z_scr, hl_scr, s_scr, w_scr):
    n_group = q_ref.shape[0]
    n_pairs = q_ref.shape[2] // LANES
    units = [(g, p) for g in range(n_group) for p in range(n_pairs)]
    n_units = len(units)
    n_ahead = SB_STATIC_BLOCKS - 1

    row2 = lax.broadcasted_iota(jnp.int32, (SB_TQ, 2 * SB_TK), 0)
    col2 = lax.broadcasted_iota(jnp.int32, (SB_TQ, 2 * SB_TK), 1)
    below = (col2 & (SB_TK - 1)) < row2
    lane = lax.broadcasted_iota(jnp.int32, (SB_TK, LANES), 1)
    lo_m = jnp.where(lane < SB_HEAD_DIM, 1.0, 0.0).astype(BF16)
    hi_m = jnp.where(lane >= SB_HEAD_DIM, 1.0, 0.0).astype(BF16)

    def block_front(q_rows, j, diagonal, slot):
        off = pl.multiple_of(j * SB_TK, SB_TK)
        base = slot * n_units
        for u, (g, p) in enumerate(units):
            cs = slice(p * LANES, (p + 1) * LANES)
            kb = k_ref[g, pl.ds(off, SB_TK), cs]
            k2 = jnp.concatenate([kb * lo_m, kb * hi_m], axis=0)
            z = _dot_nt(q_ref[g, q_rows, cs], k2)
            z_scr[base + u] = z * LOG2_E
        for u in range(n_units):
            z2 = z_scr[base + u]
            sp2 = jnp.maximum(z2, 0.0) + jnp.log2(1.0 + jnp.exp2(-jnp.abs(z2)))
            if diagonal:
                sp2 = jnp.where(below, sp2, 0.0)
                z_scr[base + u] = jnp.where(below, z2, NEG_BIG)
            hi, lo = _split_bf16(sp2, 2)
            hl_scr[base + u, :, 0:2 * SB_TK] = hi
            hl_scr[base + u, :, 2 * SB_TK:4 * SB_TK] = lo
        for u in range(n_units):
            s_scr[base + u] = _dot(hl_scr[base + u], u2_ref[...])

    def block_back(j, slot):
        off = pl.multiple_of(j * SB_TK, SB_TK)
        base = slot * n_units
        rmax = None
        for u in range(n_units):
            r_prev = jnp.concatenate([rsum[2 * u], rsum[2 * u + 1]], axis=1)
            s = s_scr[base + u] + r_prev
            w_scr[base + u] = jnp.exp2(z_scr[base + u] + s).astype(BF16)
            for e in range(2):
                r_new = jnp.broadcast_to(s[:, e * SB_TK:e * SB_TK + 1], (SB_TQ, LANES))
                rsum[2 * u + e] = r_new
                rmax = r_new if rmax is None else jnp.maximum(rmax, r_new)
        for u, (g, p) in enumerate(units):
            cs = slice(p * LANES, (p + 1) * LANES)
            vb = v_ref[g, pl.ds(off, SB_TK), cs]
            v2 = jnp.concatenate([vb * lo_m, vb * hi_m], axis=0)
            acc[u] += _dot(w_scr[base + u], v2)
        return jnp.max(rmax)

    def walk_rest(q_rows, j_start, rmax_start):
        def cond(carry):
            j, rmax = carry
            return jnp.logical_and(j >= 0, rmax > SB_SKIP_BELOW * LOG2_E)

        def body(carry):
            j, _ = carry
            block_front(q_rows, j, False, 0)
            return j - 1, block_back(j, 0)

        lax.while_loop(cond, body, (j_start, rmax_start))

    def query_block(qb, carry):
        qi = pl.program_id(1) * SB_QB_PER_STEP + qb
        q_rows = pl.ds(pl.multiple_of(qb * SB_TQ, SB_TQ), SB_TQ)
        acc[...] = jnp.zeros(acc.shape, F32)
        rsum[...] = jnp.zeros(rsum.shape, F32)

        @pl.when(qi >= n_ahead)
        def _():
            for b in range(SB_STATIC_BLOCKS):
                block_front(q_rows, qi - b, b == 0, b)
            for b in range(SB_STATIC_BLOCKS):
                rmax = block_back(qi - b, b)
            walk_rest(q_rows, qi - SB_STATIC_BLOCKS, rmax)

        @pl.when(qi < n_ahead)
        def _():
            block_front(q_rows, qi, True, 0)
            walk_rest(q_rows, qi - 1, block_back(qi, 0))

        for g in range(n_group):
            o = jnp.concatenate([acc[g * n_pairs + p] for p in range(n_pairs)], axis=1)
            o_ref[g, q_rows, :] = _rms(o, g_ref[...]).astype(BF16)
        return carry

    lax.fori_loop(0, SB_QB_PER_STEP, query_block, 0)


def _sb_attention(qkv, u2, g, batch):
    t, n3 = qkv.shape
    n = n3 // 3
    s = t // batch
    tq = SB_QB_PER_STEP * SB_TQ
    nq = s // tq
    n_pairs = n // LANES
    group = SB_GROUP if batch % SB_GROUP == 0 else 1
    n_units = group * n_pairs
    qkv3 = qkv.reshape(batch, s, n3)
    out = pl.pallas_call(
        _sb_kernel,
        grid=(batch // group, nq),
        in_specs=[
            pl.BlockSpec((group, tq, n), lambda b, i: (b, i, 0)),
            pl.BlockSpec((group, s, n), lambda b, i: (b, 0, 1)),
            pl.BlockSpec((group, s, n), lambda b, i: (b, 0, 2)),
            pl.BlockSpec((4 * SB_TK, 2 * SB_TK), lambda b, i: (0, 0)),
            pl.BlockSpec((1, n), lambda b, i: (0, 0)),
        ],
        out_specs=pl.BlockSpec((group, tq, n), lambda b, i: (b, i, 0)),
        out_shape=jax.ShapeDtypeStruct((batch, s, n), BF16),
        scratch_shapes=[
            pltpu.VMEM((n_units, SB_TQ, LANES), F32),
            pltpu.VMEM((2 * n_units, SB_TQ, LANES), F32),
            pltpu.VMEM((SB_STATIC_BLOCKS * n_units, SB_TQ, 2 * SB_TK), F32),
            pltpu.VMEM((SB_STATIC_BLOCKS * n_units, SB_TQ, 4 * SB_TK), BF16),
            pltpu.VMEM((SB_STATIC_BLOCKS * n_units, SB_TQ, 2 * SB_TK), F32),
            pltpu.VMEM((SB_STATIC_BLOCKS * n_units, SB_TQ, 2 * SB_TK), BF16),
        ],
        compiler_params=_cparams(("parallel", "arbitrary")),
        name="sb_attn",
    )(qkv3, qkv3, qkv3, u2, g)
    return out.reshape(t, n)


def _memkv_kernel(m_ref, g_ref, wk_ref, wv_ref, k_ref, v_ref):
    mn = _rms(m_ref[...], g_ref[...]).astype(BF16)
    k_ref[...] = _dot(mn, wk_ref[...]).astype(BF16)
    v_ref[...] = _dot(mn, wv_ref[...]).astype(BF16)


def _mem_kv(mem2d, g, wk, wv, layer, batch):
    tmem, d = mem2d.shape
    m = tmem // batch
    n = wk.shape[2]
    const = lambda b: (0, 0)
    row = lambda b: (b, 0)
    return pl.pallas_call(
        _memkv_kernel,
        grid=(batch,),
        in_specs=[
            pl.BlockSpec((m, d), row),
            pl.BlockSpec((1, d), const),
            pl.BlockSpec((None, d, n), lambda b: (layer, 0, 0)),
            pl.BlockSpec((None, d, n), lambda b: (layer, 0, 0)),
        ],
        out_specs=[pl.BlockSpec((m, n), row), pl.BlockSpec((m, n), row)],
        out_shape=[jax.ShapeDtypeStruct((tmem, n), BF16)] * 2,
        compiler_params=_cparams(("parallel",)),
        name="mem_kv",
    )(mem2d, g, wk, wv)


def _tail_kernel(h_ref, ya_ref, yb_ref, wa_ref, wb_ref, gx_ref, wq_ref, k_ref, v_ref, wo_ref, gf_ref, w1_ref, w2_ref,
                 gfin_ref, o_ref, *, ff_chunk, final_norm):
    h = h_ref[...] + _dot(ya_ref[...], wa_ref[...]) + _dot(yb_ref[...], wb_ref[...])

    hn = _rms(h, gx_ref[...]).astype(BF16)
    q = _dot(hn, wq_ref[...]).astype(BF16)
    scale = 1.0 / math.sqrt(XA_HEAD_DIM)
    outs = []
    for hd in range(q.shape[1] // XA_HEAD_DIM):
        cs = slice(hd * XA_HEAD_DIM, (hd + 1) * XA_HEAD_DIM)
        logits = _dot_nt(q[:, cs], k_ref[:, cs]) * scale
        e = jnp.exp(logits - jnp.max(logits, axis=-1, keepdims=True))
        denom = jnp.sum(e, axis=-1, keepdims=True)
        outs.append(_dot(e.astype(BF16), v_ref[:, cs]) / denom)
    h = h + _dot(jnp.concatenate(outs, axis=1).astype(BF16), wo_ref[...])

    hn = _rms(h, gf_ref[...]).astype(BF16)
    out = h
    for c in range(w1_ref.shape[1] // ff_chunk):
        cs = slice(c * ff_chunk, (c + 1) * ff_chunk)
        u = jnp.maximum(_dot(hn, w1_ref[:, cs]), 0.0)
        out = out + _dot((u * u).astype(BF16), w2_ref[cs, :])
    if final_norm:
        out = _rms(out, gfin_ref[...])
    o_ref[...] = out


def _layer_tail(h, ya, yb, w_out, gx, wq, k, v, wo, gf, w1, w2, gfin, layer, batch, final_norm, tm=1024, ff_chunk=1024):
    t, d = h.shape
    na, nb = ya.shape[1], yb.shape[1]
    assert na == nb
    n = wq.shape[2]
    f = w1.shape[2]
    m = k.shape[0] // batch
    tiles_per_batch = t // batch // tm
    row = lambda i: (i, 0)
    kv = lambda i: (i // tiles_per_batch, 0)
    resident = lambda shape: pl.BlockSpec(shape, lambda i: (0, 0), pipeline_mode=pl.Buffered(1))
    weight = lambda shape, r=0: pl.BlockSpec((None,) + shape, lambda i: (layer, r, 0), pipeline_mode=pl.Buffered(1))
    return pl.pallas_call(
        functools.partial(_tail_kernel, ff_chunk=ff_chunk, final_norm=final_norm),
        grid=(t // tm,),
        in_specs=[
            pl.BlockSpec((tm, d), row),
            pl.BlockSpec((tm, na), row),
            pl.BlockSpec((tm, nb), row),
            weight((na, d), 0),
            weight((nb, d), 1),
            resident((1, d)),
            weight((d, n)),
            pl.BlockSpec((m, n), kv),
            pl.BlockSpec((m, n), kv),
            weight((n, d)),
            resident((1, d)),
            weight((d, f)),
            weight((f, d)),
            resident((1, d)),
        ],
        out_specs=pl.BlockSpec((tm, d), row),
        out_shape=jax.ShapeDtypeStruct((t, d), F32),
        compiler_params=_cparams(("parallel",)),
        name="layer_tail",
    )(h, ya, yb, w_out, w_out, gx, wq, k, v, wo, gf, w1, w2, gfin)


def _pad_lanes(v):
    return jnp.pad(v.astype(F32), (0, LANES - v.shape[0])).reshape(1, LANES)


def kernel(x, mem, norm_mix_g, w_in, conv_w, conv_b, dt_bias, a_log, d_skip, ssd_norm_g, sb_norm_g, w_out,
           norm_xa_g, norm_mem_g, w_xq, w_xk, w_xv, w_xo, norm_ff_g, w_ff1, w_ff2, final_g):
    batch, seq, d = x.shape
    depth = w_in.shape[0]
    n_heads = dt_bias.shape[1]
    t = batch * seq
    row = lambda v: v.astype(F32).reshape(1, -1)

    jj = lax.broadcasted_iota(jnp.int32, (2 * SB_TK, 2 * SB_TK), 0)
    ss = lax.broadcasted_iota(jnp.int32, (2 * SB_TK, 2 * SB_TK), 1)
    u2 = -((jj // SB_TK == ss // SB_TK) & (jj >= ss)).astype(BF16)
    u2 = jnp.concatenate([u2, u2], axis=0)

    w_out_b, w_xq_b, w_xk_b, w_xv_b, w_xo_b, w_ff1_b, w_ff2_b = (
        w.astype(BF16) for w in (w_out, w_xq, w_xk, w_xv, w_xo, w_ff1, w_ff2))
    w_in = jnp.swapaxes(w_in.astype(F32), 1, 2)

    h = x.reshape(t, d)
    mem2d = mem.reshape(batch * mem.shape[1], d)
    for l in range(depth):
        qkv, y_ssd = _proj_ssd(h, row(norm_mix_g[l]), w_in, l, conv_w[l].astype(F32)[:, None, :],
                               row(conv_b[l]), _pad_lanes(dt_bias[l]), _pad_lanes(a_log[l]),
                               row(jnp.repeat(d_skip[l], SSD_HEAD_DIM)), row(ssd_norm_g[l]), batch, n_heads)
        y_sb = _sb_attention(qkv, u2, row(sb_norm_g[l]), batch)
        k_mem, v_mem = _mem_kv(mem2d, row(norm_mem_g[l]), w_xk_b, w_xv_b, l, batch)
        h = _layer_tail(h, y_ssd, y_sb, w_out_b, row(norm_xa_g[l]), w_xq_b, k_mem, v_mem, w_xo_b,
                        row(norm_ff_g[l]), w_ff1_b, w_ff2_b, row(final_g), l, batch, final_norm=(l == depth - 1))
    return h.reshape(batch, seq, d)
```

```python
import functools
import math

import jax
import jax.numpy as jnp
from jax import lax
from jax.experimental import pallas as pl
from jax.experimental.pallas import tpu as pltpu

F32 = jnp.float32
BF16 = jnp.bfloat16

EPS = 1e-5
CONV_K = 4
CHUNK = 128
SSD_HEAD_DIM = 64
SSD_GROUPS = 2
SSD_STATE = 64
SB_HEAD_DIM = 64
XA_HEAD_DIM = 128

LANES = 128
SUBLANES = 8
MXU_N = 256
VMEM_LIMIT = 56 * 1024 * 1024

SB_TQ = 128
SB_TK = 128
SB_GROUP = 2
SB_STATIC_BLOCKS = 3
SB_QB_PER_STEP = 4
SB_SKIP_BELOW = -110.0
NEG_BIG = -1e30
LOG2_E = math.log2(math.e)


def _cparams(sem):
    return pltpu.CompilerParams(dimension_semantics=sem, vmem_limit_bytes=VMEM_LIMIT)


def _rms(x, g):
    return x * lax.rsqrt(jnp.mean(x * x, axis=-1, keepdims=True) + EPS) * g


def _softplus(x):
    return jnp.maximum(x, 0.0) + jnp.log(1.0 + jnp.exp(-jnp.abs(x)))


def _dot(a, b):
    return jnp.dot(a, b, preferred_element_type=F32)


def _dot_nt(a, b):
    return lax.dot_general(a, b, (((1,), (1,)), ((), ())), preferred_element_type=F32)


def _split_bf16(x, parts):
    out = []
    r = x
    for _ in range(parts - 1):
        p = r.astype(BF16)
        out.append(p)
        r = r - p.astype(F32)
    out.append(r.astype(BF16))
    return out


def _expand_heads(a, lane):
    cols = []
    for j in range(4):
        lo = jnp.broadcast_to(a[:, 2 * j:2 * j + 1], (a.shape[0], LANES))
        hi = jnp.broadcast_to(a[:, 2 * j + 1:2 * j + 2], (a.shape[0], LANES))
        cols.append(jnp.where(lane < SSD_HEAD_DIM, lo, hi))
    return jnp.concatenate(cols, axis=1)


def _ssd_chunk(z, window, dt_raw, st, cw_ref, cb_ref, dtb_ref, alog_ref, dsk_ref, gn_ref, interleave):
    n_inner = z.shape[1]
    halo = window.shape[0] - CHUNK
    interleave = list(interleave)
    per_stage = -(-len(interleave) // 3)

    def run_interleaved(stage):
        for job in interleave[stage * per_stage:(stage + 1) * per_stage]:
            job()

    run_interleaved(0)

    conv = cb_ref[...] + cw_ref[CONV_K - 1] * window[halo:, :]
    for k in range(CONV_K - 1):
        shifted = pltpu.roll(window, CONV_K - 1 - k, 0)[halo:, :]
        conv = conv + cw_ref[k] * shifted
    xbc = conv * jax.nn.sigmoid(conv)
    xs = xbc[:, :n_inner]
    bm = xbc[:, n_inner:n_inner + LANES]
    cm = xbc[:, n_inner + LANES:]

    row = lax.broadcasted_iota(jnp.int32, (CHUNK, CHUNK), 0)
    col = lax.broadcasted_iota(jnp.int32, (CHUNK, CHUNK), 1)
    causal = row >= col
    lane = col

    dt = _softplus(dt_raw + dtb_ref[...])
    a_c = dt * (-jnp.exp(alog_ref[...]))
    tri = jnp.where(causal, 1.0, 0.0).astype(BF16)
    a_cum = None
    for part in _split_bf16(a_c, 3):
        term = _dot(tri, part)
        a_cum = term if a_cum is None else a_cum + term
    a_cum_t = a_cum.T

    dt_x = _expand_heads(dt, lane)
    acum_x = _expand_heads(a_cum, lane)
    last_x = acum_x[CHUNK - 1:CHUNK, :]
    xc = xs * dt_x
    e_acum = jnp.exp(acum_x)
    xd = (xc * jnp.exp(last_x - acum_x)).astype(BF16)
    chunk_decay = jnp.exp(last_x)
    run_interleaved(1)

    bm_b = bm.astype(BF16)
    cbs = []
    for g in range(SSD_GROUPS):
        in_g = (lane >= g * SSD_STATE) & (lane < (g + 1) * SSD_STATE)
        cbs.append(_dot_nt(jnp.where(in_g, cm, 0.0).astype(BF16), bm_b))
    y_cols = []
    for j in range(n_inner // LANES):
        ms = []
        for h in (2 * j, 2 * j + 1):
            seg = a_cum[:, h:h + 1] - a_cum_t[h:h + 1, :]
            decay = jnp.exp(jnp.where(causal, seg, NEG_BIG))
            ms.append((cbs[h * SSD_HEAD_DIM * SSD_GROUPS // n_inner] * decay).astype(BF16))
        xcj = xc[:, j * LANES:(j + 1) * LANES]
        x_lo = jnp.where(lane < SSD_HEAD_DIM, xcj, 0.0).astype(BF16)
        x_hi = jnp.where(lane >= SSD_HEAD_DIM, xcj, 0.0).astype(BF16)
        y_cols.append(_dot(jnp.concatenate(ms, axis=1), jnp.concatenate([x_lo, x_hi], axis=0)))
    y = jnp.concatenate(y_cols, axis=1)
    run_interleaved(2)

    y = y + _dot(cm.astype(BF16), st.astype(BF16)) * e_acum
    srow = lax.broadcasted_iota(jnp.int32, st.shape, 0)
    scol = lax.broadcasted_iota(jnp.int32, st.shape, 1)
    same_group = (srow // SSD_STATE) == (scol // (n_inner // SSD_GROUPS))
    new_st = st * chunk_decay + jnp.where(same_group, _dot(bm.T.astype(BF16), xd), 0.0)

    y = y + dsk_ref[...] * xs
    y = y * (z * jax.nn.sigmoid(z))
    return _rms(y, gn_ref[...]).astype(BF16), new_st


def _proj_ssd_kernel(h_ref, g_ref, win_ref, cw_ref, cb_ref, dtb_ref, alog_ref, dsk_ref, gn_ref,
                     mem_ref, gm_ref, wk_ref, wv_ref,
                     qkv_ref, y_ref, km_ref, vm_ref,
                     wzx_ref, wdt_ref, wqkv_ref, zx_a, dt_a, zx_b, dt_b, halo_scr, state,
                     *, tiles_per_batch, n_heads):
    s = pl.program_id(0)
    tm = h_ref.shape[0]
    n_inner = y_ref.shape[1]

    @pl.when(s == 0)
    def _():
        zx_b[...] = jnp.zeros(zx_b.shape, F32)
        dt_b[...] = jnp.zeros(dt_b.shape, F32)
        halo_scr[...] = jnp.zeros(halo_scr.shape, F32)
        state[...] = jnp.zeros(state.shape, F32)
        nzx, nqkv = wzx_ref.shape[0], wqkv_ref.shape[0]
        n_sb = nqkv // 3
        q_scale = 1.0 / math.sqrt(SB_HEAD_DIM)
        for r in range(0, nzx, CHUNK):
            wzx_ref[r:r + CHUNK, :] = win_ref[r:r + CHUNK, :].astype(BF16)
        row = lax.broadcasted_iota(jnp.int32, wdt_ref.shape, 0)
        wdt_ref[...] = jnp.where(row < n_heads, win_ref[nzx:nzx + wdt_ref.shape[0], :], 0.0).astype(BF16)
        for r in range(0, nqkv, CHUNK):
            blk = win_ref[nzx + n_heads + r:nzx + n_heads + r + CHUNK, :]
            wqkv_ref[r:r + CHUNK, :] = (blk * q_scale if r < n_sb else blk).astype(BF16)

    @pl.when(lax.rem(s, tiles_per_batch) == 0)
    def _():
        mn = _rms(mem_ref[...], gm_ref[...]).astype(BF16)
        km_ref[...] = _dot(mn, wk_ref[...].astype(BF16)).astype(BF16)
        vm_ref[...] = _dot(mn, wv_ref[...].astype(BF16)).astype(BF16)

    def step(zx_w, dt_w, zx_r, dt_r):
        hn = _rms(h_ref[...], g_ref[...]).astype(BF16)
        n_chunks = tm // CHUNK
        nzx, nqkv = zx_w.shape[1], qkv_ref.shape[1]

        def project(out_ref, w_ref, a):
            b = min(a + MXU_N, w_ref.shape[0])
            out_ref[:, a:b] = _dot_nt(hn, w_ref[a:b, :]).astype(out_ref.dtype)

        jobs = [functools.partial(project, zx_w, wzx_ref, a) for a in range(0, nzx, MXU_N)]
        jobs += [functools.partial(project, dt_w, wdt_ref, 0)]
        jobs += [functools.partial(project, qkv_ref, wqkv_ref, a) for a in range(0, nqkv, MXU_N)]
        share = -(-len(jobs) // n_chunks)

        fresh = lax.rem(s - 1, tiles_per_batch) == 0
        st = jnp.where(fresh, 0.0, state[...])
        halo = jnp.where(fresh, 0.0, halo_scr[...])
        for c in range(n_chunks):
            r0 = c * CHUNK
            if c == 0:
                window = jnp.concatenate([halo, zx_r[0:CHUNK, n_inner:]], axis=0)
            else:
                window = zx_r[r0 - SUBLANES:r0 + CHUNK, n_inner:]
            y, st = _ssd_chunk(zx_r[r0:r0 + CHUNK, :n_inner], window, dt_r[r0:r0 + CHUNK, :], st,
                               cw_ref, cb_ref, dtb_ref, alog_ref, dsk_ref, gn_ref, jobs[c * share:(c + 1) * share])
            y_ref[r0:r0 + CHUNK, :] = y
        state[...] = st
        halo_scr[...] = zx_r[tm - SUBLANES:tm, n_inner:]

    @pl.when(s % 2 == 0)
    def _():
        step(zx_a, dt_a, zx_b, dt_b)

    @pl.when(s % 2 == 1)
    def _():
        step(zx_b, dt_b, zx_a, dt_a)


def _proj_ssd(h, g, w_in, layer, cw, cb, dtb, alog, dsk, gn, mem2d, gm, wk, wv, batch, n_heads, tm=512):
    t, d = h.shape
    n_inner = gn.shape[1]
    n_conv = cb.shape[1]
    nzx, ndt = n_inner + n_conv, LANES
    in_dim = w_in.shape[1]
    nqkv = in_dim - nzx - n_heads
    n_tiles = t // tm
    last = n_tiles - 1
    tpb = n_tiles // batch
    m = mem2d.shape[0] // batch
    nx = wk.shape[2]
    seq_of = lambda s: (jnp.minimum(s, last) // tpb, 0)
    resident = lambda shape: pl.BlockSpec(shape, lambda s: (0,) * len(shape), pipeline_mode=pl.Buffered(1))
    return pl.pallas_call(
        functools.partial(_proj_ssd_kernel, tiles_per_batch=tpb, n_heads=n_heads),
        grid=(n_tiles + 1,),
        in_specs=[
            pl.BlockSpec((tm, d), lambda s: (jnp.minimum(s, last), 0)),
            resident((1, d)),
            pl.BlockSpec((None, in_dim, d), lambda s: (layer, 0, 0), pipeline_mode=pl.Buffered(1)),
            resident((CONV_K, 1, n_conv)),
            resident((1, n_conv)),
            resident((1, LANES)),
            resident((1, LANES)),
            resident((1, n_inner)),
            resident((1, n_inner)),
            pl.BlockSpec((m, d), seq_of),
            resident((1, d)),
            pl.BlockSpec((None, d, nx), lambda s: (layer, 0, 0), pipeline_mode=pl.Buffered(1)),
            pl.BlockSpec((None, d, nx), lambda s: (layer, 0, 0), pipeline_mode=pl.Buffered(1)),
        ],
        out_specs=[
            pl.BlockSpec((tm, nqkv), lambda s: (jnp.minimum(s, last), 0)),
            pl.BlockSpec((tm, n_inner), lambda s: (jnp.maximum(s - 1, 0), 0)),
            pl.BlockSpec((m, nx), seq_of),
            pl.BlockSpec((m, nx), seq_of),
        ],
        out_shape=[
            jax.ShapeDtypeStruct((t, nqkv), BF16),
            jax.ShapeDtypeStruct((t, n_inner), BF16),
            jax.ShapeDtypeStruct((batch * m, nx), BF16),
            jax.ShapeDtypeStruct((batch * m, nx), BF16),
        ],
        scratch_shapes=[
            pltpu.VMEM((nzx, d), BF16),
            pltpu.VMEM((ndt, d), BF16),
            pltpu.VMEM((nqkv, d), BF16),
            pltpu.VMEM((tm, nzx), F32),
            pltpu.VMEM((tm, ndt), F32),
            pltpu.VMEM((tm, nzx), F32),
            pltpu.VMEM((tm, ndt), F32),
            pltpu.VMEM((SUBLANES, n_conv), F32),
            pltpu.VMEM((SSD_GROUPS * SSD_STATE, n_inner), F32),
        ],
        compiler_params=_cparams(("arbitrary",)),
        name="proj_ssd",
    )(h, g, w_in, cw, cb, dtb, alog, dsk, gn, mem2d, gm, wk, wv)


def _sb_kernel(q_ref, k_ref, v_ref, u2_ref, g_ref, o_ref, acc, rsum, z_scr, hl_scr, s_scr, w_scr):
    n_group = q_ref.shape[0]
    n_pairs = q_ref.shape[2] // LANES
    units = [(g, p) for g in range(n_group) for p in range(n_pairs)]
    n_units = len(units)
    n_ahead = SB_STATIC_BLOCKS - 1

    row2 = lax.broadcasted_iota(jnp.int32, (SB_TQ, 2 * SB_TK), 0)
    col2 = lax.broadcasted_iota(jnp.int32, (SB_TQ, 2 * SB_TK), 1)
    below = (col2 & (SB_TK - 1)) < row2
    lane = lax.broadcasted_iota(jnp.int32, (SB_TK, LANES), 1)
    lo_m = jnp.where(lane < SB_HEAD_DIM, 1.0, 0.0).astype(BF16)
    hi_m = jnp.where(lane >= SB_HEAD_DIM, 1.0, 0.0).astype(BF16)

    def block_front(q_rows, j, diagonal, slot):
        off = pl.multiple_of(j * SB_TK, SB_TK)
        base = slot * n_units
        for u, (g, p) in enumerate(units):
            cs = slice(p * LANES, (p + 1) * LANES)
            kb = k_ref[g, pl.ds(off, SB_TK), cs]
            k2 = jnp.concatenate([kb * lo_m, kb * hi_m], axis=0)
            z = _dot_nt(q_ref[g, q_rows, cs], k2)
            z_scr[base + u] = z * LOG2_E
        for u in range(n_units):
            z2 = z_scr[base + u]
            sp2 = jnp.maximum(z2, 0.0) + jnp.log2(1.0 + jnp.exp2(-jnp.abs(z2)))
            if diagonal:
                sp2 = jnp.where(below, sp2, 0.0)
                z_scr[base + u] = jnp.where(below, z2, NEG_BIG)
            hi, lo = _split_bf16(sp2, 2)
            hl_scr[base + u, :, 0:2 * SB_TK] = hi
            hl_scr[base + u, :, 2 * SB_TK:4 * SB_TK] = lo
        for u in range(n_units):
            s_scr[base + u] = _dot(hl_scr[base + u], u2_ref[...])

    def block_back(j, slot):
        off = pl.multiple_of(j * SB_TK, SB_TK)
        base = slot * n_units
        rmax = None
        for u in range(n_units):
            r_prev = jnp.concatenate([rsum[2 * u], rsum[2 * u + 1]], axis=1)
            s = s_scr[base + u] + r_prev
            w_scr[base + u] = jnp.exp2(z_scr[base + u] + s).astype(BF16)
            for e in range(2):
                r_new = jnp.broadcast_to(s[:, e * SB_TK:e * SB_TK + 1], (SB_TQ, LANES))
                rsum[2 * u + e] = r_new
                rmax = r_new if rmax is None else jnp.maximum(rmax, r_new)
        for u, (g, p) in enumerate(units):
            cs = slice(p * LANES, (p + 1) * LANES)
            vb = v_ref[g, pl.ds(off, SB_TK), cs]
            v2 = jnp.concatenate([vb * lo_m, vb * hi_m], axis=0)
            acc[u] += _dot(w_scr[base + u], v2)
        return jnp.max(rmax)

    def walk_rest(q_rows, j_start, rmax_start):
        def cond(carry):
            j, rmax = carry
            return jnp.logical_and(j >= 0, rmax > SB_SKIP_BELOW * LOG2_E)

        def body(carry):
            j, _ = carry
            block_front(q_rows, j, False, 0)
            return j - 1, block_back(j, 0)

        lax.while_loop(cond, body, (j_start, rmax_start))

    def query_block(qb, carry):
        qi = pl.program_id(1) * SB_QB_PER_STEP + qb
        q_rows = pl.ds(pl.multiple_of(qb * SB_TQ, SB_TQ), SB_TQ)
        acc[...] = jnp.zeros(acc.shape, F32)
        rsum[...] = jnp.zeros(rsum.shape, F32)

        @pl.when(qi >= n_ahead)
        def _():
            for b in range(SB_STATIC_BLOCKS):
                block_front(q_rows, qi - b, b == 0, b)
            for b in range(SB_STATIC_BLOCKS):
                rmax = block_back(qi - b, b)
            walk_rest(q_rows, qi - SB_STATIC_BLOCKS, rmax)

        @pl.when(qi < n_ahead)
        def _():
            block_front(q_rows, qi, True, 0)
            walk_rest(q_rows, qi - 1, block_back(qi, 0))

        for g in range(n_group):
            o = jnp.concatenate([acc[g * n_pairs + p] for p in range(n_pairs)], axis=1)
            o_ref[g, q_rows, :] = _rms(o, g_ref[...]).astype(BF16)
        return carry

    lax.fori_loop(0, SB_QB_PER_STEP, query_block, 0)


def _sb_attention(qkv, u2, g, batch):
    t, n3 = qkv.shape
    n = n3 // 3
    s = t // batch
    tq = SB_QB_PER_STEP * SB_TQ
    nq = s // tq
    n_pairs = n // LANES
    group = SB_GROUP if batch % SB_GROUP == 0 else 1
    n_units = group * n_pairs
    qkv3 = qkv.reshape(batch, s, n3)
    out = pl.pallas_call(
        _sb_kernel,
        grid=(batch // group, nq),
        in_specs=[
            pl.BlockSpec((group, tq, n), lambda b, i: (b, i, 0)),
            pl.BlockSpec((group, s, n), lambda b, i: (b, 0, 1)),
            pl.BlockSpec((group, s, n), lambda b, i: (b, 0, 2)),
            pl.BlockSpec((4 * SB_TK, 2 * SB_TK), lambda b, i: (0, 0)),
            pl.BlockSpec((1, n), lambda b, i: (0, 0)),
        ],
        out_specs=pl.BlockSpec((group, tq, n), lambda b, i: (b, i, 0)),
        out_shape=jax.ShapeDtypeStruct((batch, s, n), BF16),
        scratch_shapes=[
            pltpu.VMEM((n_units, SB_TQ, LANES), F32),
            pltpu.VMEM((2 * n_units, SB_TQ, LANES), F32),
            pltpu.VMEM((SB_STATIC_BLOCKS * n_units, SB_TQ, 2 * SB_TK), F32),
            pltpu.VMEM((SB_STATIC_BLOCKS * n_units, SB_TQ, 4 * SB_TK), BF16),
            pltpu.VMEM((SB_STATIC_BLOCKS * n_units, SB_TQ, 2 * SB_TK), F32),
            pltpu.VMEM((SB_STATIC_BLOCKS * n_units, SB_TQ, 2 * SB_TK), BF16),
        ],
        compiler_params=_cparams(("parallel", "arbitrary")),
        name="sb_attn",
    )(qkv3, qkv3, qkv3, u2, g)
    return out.reshape(t, n)


def _tail_kernel(h_ref, ya_ref, yb_ref, wa_ref, wb_ref, gx_ref, wq_ref, k_ref, v_ref, wo_ref, gf_ref, w1_ref, w2_ref,
                 gfin_ref, o_ref, *, ff_chunk, final_norm):
    h = h_ref[...] + _dot(ya_ref[...], wa_ref[...]) + _dot(yb_ref[...], wb_ref[...])

    hn = _rms(h, gx_ref[...]).astype(BF16)
    q = _dot(hn, wq_ref[...]).astype(BF16)
    scale = 1.0 / math.sqrt(XA_HEAD_DIM)
    outs = []
    for hd in range(q.shape[1] // XA_HEAD_DIM):
        cs = slice(hd * XA_HEAD_DIM, (hd + 1) * XA_HEAD_DIM)
        logits = _dot_nt(q[:, cs], k_ref[:, cs]) * scale
        e = jnp.exp(logits - jnp.max(logits, axis=-1, keepdims=True))
        denom = jnp.sum(e, axis=-1, keepdims=True)
        outs.append(_dot(e.astype(BF16), v_ref[:, cs]) / denom)
    h = h + _dot(jnp.concatenate(outs, axis=1).astype(BF16), wo_ref[...])

    hn = _rms(h, gf_ref[...]).astype(BF16)
    out = h
    for c in range(w1_ref.shape[1] // ff_chunk):
        cs = slice(c * ff_chunk, (c + 1) * ff_chunk)
        u = jnp.maximum(_dot(hn, w1_ref[:, cs]), 0.0)
        out = out + _dot((u * u).astype(BF16), w2_ref[cs, :])
    if final_norm:
        out = _rms(out, gfin_ref[...])
    o_ref[...] = out


def _layer_tail(h, ya, yb, w_out, gx, wq, k, v, wo, gf, w1, w2, gfin, layer, batch, final_norm, tm=1024, ff_chunk=1024):
    t, d = h.shape
    na, nb = ya.shape[1], yb.shape[1]
    assert na == nb
    n = wq.shape[2]
    f = w1.shape[2]
    m = k.shape[0] // batch
    tiles_per_batch = t // batch // tm
    row = lambda i: (i, 0)
    kv = lambda i: (i // tiles_per_batch, 0)
    resident = lambda shape: pl.BlockSpec(shape, lambda i: (0, 0), pipeline_mode=pl.Buffered(1))
    weight = lambda shape, r=0: pl.BlockSpec((None,) + shape, lambda i: (layer, r, 0), pipeline_mode=pl.Buffered(1))
    return pl.pallas_call(
        functools.partial(_tail_kernel, ff_chunk=ff_chunk, final_norm=final_norm),
        grid=(t // tm,),
        in_specs=[
            pl.BlockSpec((tm, d), row),
            pl.BlockSpec((tm, na), row),
            pl.BlockSpec((tm, nb), row),
            weight((na, d), 0),
            weight((nb, d), 1),
            resident((1, d)),
            weight((d, n)),
            pl.BlockSpec((m, n), kv),
            pl.BlockSpec((m, n), kv),
            weight((n, d)),
            resident((1, d)),
            weight((d, f)),
            weight((f, d)),
            resident((1, d)),
        ],
        out_specs=pl.BlockSpec((tm, d), row),
        out_shape=jax.ShapeDtypeStruct((t, d), F32),
        compiler_params=_cparams(("parallel",)),
        name="layer_tail",
    )(h, ya, yb, w_out, w_out, gx, wq, k, v, wo, gf, w1, w2, gfin)


def _pad_lanes(v):
    return jnp.pad(v.astype(F32), (0, LANES - v.shape[0])).reshape(1, LANES)


def kernel(x, mem, norm_mix_g, w_in, conv_w, conv_b, dt_bias, a_log, d_skip, ssd_norm_g, sb_norm_g, w_out,
           norm_xa_g, norm_mem_g, w_xq, w_xk, w_xv, w_xo, norm_ff_g, w_ff1, w_ff2, final_g):
    batch, seq, d = x.shape
    depth = w_in.shape[0]
    n_heads = dt_bias.shape[1]
    t = batch * seq
    row = lambda v: v.astype(F32).reshape(1, -1)

    jj = lax.broadcasted_iota(jnp.int32, (2 * SB_TK, 2 * SB_TK), 0)
    ss = lax.broadcasted_iota(jnp.int32, (2 * SB_TK, 2 * SB_TK), 1)
    u2 = -((jj // SB_TK == ss // SB_TK) & (jj >= ss)).astype(BF16)
    u2 = jnp.concatenate([u2, u2], axis=0)

    w_out_b, w_xq_b, w_xo_b, w_ff1_b, w_ff2_b = (w.astype(BF16) for w in (w_out, w_xq, w_xo, w_ff1, w_ff2))
    w_in = jnp.swapaxes(w_in.astype(F32), 1, 2)

    h = x.reshape(t, d)
    mem2d = mem.reshape(batch * mem.shape[1], d)
    for l in range(depth):
        qkv, y_ssd, k_mem, v_mem = _proj_ssd(
            h, row(norm_mix_g[l]), w_in, l, conv_w[l].astype(F32)[:, None, :], row(conv_b[l]),
            _pad_lanes(dt_bias[l]), _pad_lanes(a_log[l]), row(jnp.repeat(d_skip[l], SSD_HEAD_DIM)),
            row(ssd_norm_g[l]), mem2d, row(norm_mem_g[l]), w_xk.astype(F32), w_xv.astype(F32), batch, n_heads)
        y_sb = _sb_attention(qkv, u2, row(sb_norm_g[l]), batch)
        h = _layer_tail(h, y_ssd, y_sb, w_out_b, row(norm_xa_g[l]), w_xq_b, k_mem, v_mem, w_xo_b,
                        row(norm_ff_g[l]), w_ff1_b, w_ff2_b, row(final_g), l, batch, final_norm=(l == depth - 1))
    return h.reshape(batch, seq, d)
```

```python
import functools
import math

import jax
import jax.numpy as jnp
from jax import lax
from jax.experimental import pallas as pl
from jax.experimental.pallas import tpu as pltpu

F32 = jnp.float32
BF16 = jnp.bfloat16

EPS = 1e-5
CONV_K = 4
CHUNK = 128
SSD_HEAD_DIM = 64
SSD_GROUPS = 2
SSD_STATE = 64
SB_HEAD_DIM = 64
XA_HEAD_DIM = 128

LANES = 128
SUBLANES = 8
MXU_N = 256
VMEM_LIMIT = 56 * 1024 * 1024

SB_TQ = 128
SB_TK = 128
SB_GROUP = 2
SB_HALF = SB_TK // 2
SB_STATIC_BLOCKS = 3
SB_QB_PER_STEP = 4
SB_SKIP_BELOW = -110.0
NEG_BIG = -1e30
LOG2_E = math.log2(math.e)


def _cparams(sem):
    return pltpu.CompilerParams(dimension_semantics=sem, vmem_limit_bytes=VMEM_LIMIT)


def _rms(x, g):
    return x * lax.rsqrt(jnp.mean(x * x, axis=-1, keepdims=True) + EPS) * g


def _softplus(x):
    return jnp.maximum(x, 0.0) + jnp.log(1.0 + jnp.exp(-jnp.abs(x)))


def _dot(a, b):
    return jnp.dot(a, b, preferred_element_type=F32)


def _dot_nt(a, b):
    return lax.dot_general(a, b, (((1,), (1,)), ((), ())), preferred_element_type=F32)


def _split_bf16(x, parts):
    out = []
    r = x
    for _ in range(parts - 1):
        p = r.astype(BF16)
        out.append(p)
        r = r - p.astype(F32)
    out.append(r.astype(BF16))
    return out


def _expand_heads(a, lane):
    cols = []
    for j in range(4):
        lo = jnp.broadcast_to(a[:, 2 * j:2 * j + 1], (a.shape[0], LANES))
        hi = jnp.broadcast_to(a[:, 2 * j + 1:2 * j + 2], (a.shape[0], LANES))
        cols.append(jnp.where(lane < SSD_HEAD_DIM, lo, hi))
    return jnp.concatenate(cols, axis=1)


def _ssd_chunk(z, window, dt_raw, st, cw_ref, cb_ref, dtb_ref, alog_ref, dsk_ref, gn_ref, interleave):
    n_inner = z.shape[1]
    halo = window.shape[0] - CHUNK
    interleave = list(interleave)
    per_stage = -(-len(interleave) // 3)

    def run_interleaved(stage):
        for job in interleave[stage * per_stage:(stage + 1) * per_stage]:
            job()

    run_interleaved(0)

    conv = cb_ref[...] + cw_ref[CONV_K - 1] * window[halo:, :]
    for k in range(CONV_K - 1):
        shifted = pltpu.roll(window, CONV_K - 1 - k, 0)[halo:, :]
        conv = conv + cw_ref[k] * shifted
    xbc = conv * jax.nn.sigmoid(conv)
    xs = xbc[:, :n_inner]
    bm = xbc[:, n_inner:n_inner + LANES]
    cm = xbc[:, n_inner + LANES:]

    row = lax.broadcasted_iota(jnp.int32, (CHUNK, CHUNK), 0)
    col = lax.broadcasted_iota(jnp.int32, (CHUNK, CHUNK), 1)
    causal = row >= col
    lane = col

    dt = _softplus(dt_raw + dtb_ref[...])
    a_c = dt * (-jnp.exp(alog_ref[...]))
    tri = jnp.where(causal, 1.0, 0.0).astype(BF16)
    a_cum = None
    for part in _split_bf16(a_c, 3):
        term = _dot(tri, part)
        a_cum = term if a_cum is None else a_cum + term
    a_cum_t = a_cum.T

    dt_x = _expand_heads(dt, lane)
    acum_x = _expand_heads(a_cum, lane)
    last_x = acum_x[CHUNK - 1:CHUNK, :]
    xc = xs * dt_x
    e_acum = jnp.exp(acum_x)
    xd = (xc * jnp.exp(last_x - acum_x)).astype(BF16)
    chunk_decay = jnp.exp(last_x)
    run_interleaved(1)

    bm_b = bm.astype(BF16)
    cbs = []
    for g in range(SSD_GROUPS):
        in_g = (lane >= g * SSD_STATE) & (lane < (g + 1) * SSD_STATE)
        cbs.append(_dot_nt(jnp.where(in_g, cm, 0.0).astype(BF16), bm_b))
    y_cols = []
    for j in range(n_inner // LANES):
        ms = []
        for h in (2 * j, 2 * j + 1):
            seg = a_cum[:, h:h + 1] - a_cum_t[h:h + 1, :]
            decay = jnp.exp(jnp.where(causal, seg, NEG_BIG))
            ms.append((cbs[h * SSD_HEAD_DIM * SSD_GROUPS // n_inner] * decay).astype(BF16))
        xcj = xc[:, j * LANES:(j + 1) * LANES]
        x_lo = jnp.where(lane < SSD_HEAD_DIM, xcj, 0.0).astype(BF16)
        x_hi = jnp.where(lane >= SSD_HEAD_DIM, xcj, 0.0).astype(BF16)
        y_cols.append(_dot(jnp.concatenate(ms, axis=1), jnp.concatenate([x_lo, x_hi], axis=0)))
    y = jnp.concatenate(y_cols, axis=1)
    run_interleaved(2)

    y = y + _dot(cm.astype(BF16), st.astype(BF16)) * e_acum
    srow = lax.broadcasted_iota(jnp.int32, st.shape, 0)
    scol = lax.broadcasted_iota(jnp.int32, st.shape, 1)
    same_group = (srow // SSD_STATE) == (scol // (n_inner // SSD_GROUPS))
    new_st = st * chunk_decay + jnp.where(same_group, _dot(bm.T.astype(BF16), xd), 0.0)

    y = y + dsk_ref[...] * xs
    y = y * (z * jax.nn.sigmoid(z))
    return _rms(y, gn_ref[...]).astype(BF16), new_st


def _proj_ssd_kernel(h_ref, g_ref, win_ref, cw_ref, cb_ref, dtb_ref, alog_ref, dsk_ref, gn_ref,
                     qkv_ref, y_ref, wzx_ref, wdt_ref, wqkv_ref, zx_a, dt_a, zx_b, dt_b, halo_scr, state,
                     *, tiles_per_batch, n_heads):
    s = pl.program_id(0)
    tm = h_ref.shape[0]
    n_inner = y_ref.shape[1]

    @pl.when(s == 0)
    def _():
        zx_b[...] = jnp.zeros(zx_b.shape, F32)
        dt_b[...] = jnp.zeros(dt_b.shape, F32)
        halo_scr[...] = jnp.zeros(halo_scr.shape, F32)
        state[...] = jnp.zeros(state.shape, F32)
        nzx, nqkv = wzx_ref.shape[0], wqkv_ref.shape[0]
        n_sb = nqkv // 3
        q_scale = 1.0 / math.sqrt(SB_HEAD_DIM)
        for r in range(0, nzx, CHUNK):
            wzx_ref[r:r + CHUNK, :] = win_ref[r:r + CHUNK, :].astype(BF16)
        row = lax.broadcasted_iota(jnp.int32, wdt_ref.shape, 0)
        wdt_ref[...] = jnp.where(row < n_heads, win_ref[nzx:nzx + wdt_ref.shape[0], :], 0.0).astype(BF16)
        for r in range(0, nqkv, CHUNK):
            blk = win_ref[nzx + n_heads + r:nzx + n_heads + r + CHUNK, :]
            wqkv_ref[r:r + CHUNK, :] = (blk * q_scale if r < n_sb else blk).astype(BF16)

    def step(zx_w, dt_w, zx_r, dt_r):
        hn = _rms(h_ref[...], g_ref[...]).astype(BF16)
        n_chunks = tm // CHUNK
        nzx, nqkv = zx_w.shape[1], qkv_ref.shape[1]

        def project(out_ref, w_ref, a):
            b = min(a + MXU_N, w_ref.shape[0])
            out_ref[:, a:b] = _dot_nt(hn, w_ref[a:b, :]).astype(out_ref.dtype)

        jobs = [functools.partial(project, zx_w, wzx_ref, a) for a in range(0, nzx, MXU_N)]
        jobs += [functools.partial(project, dt_w, wdt_ref, 0)]
        jobs += [functools.partial(project, qkv_ref, wqkv_ref, a) for a in range(0, nqkv, MXU_N)]
        share = -(-len(jobs) // n_chunks)

        fresh = lax.rem(s - 1, tiles_per_batch) == 0
        st = jnp.where(fresh, 0.0, state[...])
        halo = jnp.where(fresh, 0.0, halo_scr[...])
        for c in range(n_chunks):
            r0 = c * CHUNK
            if c == 0:
                window = jnp.concatenate([halo, zx_r[0:CHUNK, n_inner:]], axis=0)
            else:
                window = zx_r[r0 - SUBLANES:r0 + CHUNK, n_inner:]
            y, st = _ssd_chunk(zx_r[r0:r0 + CHUNK, :n_inner], window, dt_r[r0:r0 + CHUNK, :], st,
                               cw_ref, cb_ref, dtb_ref, alog_ref, dsk_ref, gn_ref, jobs[c * share:(c + 1) * share])
            y_ref[r0:r0 + CHUNK, :] = y
        state[...] = st
        halo_scr[...] = zx_r[tm - SUBLANES:tm, n_inner:]

    @pl.when(s % 2 == 0)
    def _():
        step(zx_a, dt_a, zx_b, dt_b)

    @pl.when(s % 2 == 1)
    def _():
        step(zx_b, dt_b, zx_a, dt_a)


def _proj_ssd(h, g, w_in, layer, cw, cb, dtb, alog, dsk, gn, batch, n_heads, tm=512):
    t, d = h.shape
    n_inner = gn.shape[1]
    n_conv = cb.shape[1]
    nzx, ndt = n_inner + n_conv, LANES
    in_dim = w_in.shape[1]
    nqkv = in_dim - nzx - n_heads
    n_tiles = t // tm
    last = n_tiles - 1
    resident = lambda shape: pl.BlockSpec(shape, lambda s: (0,) * len(shape), pipeline_mode=pl.Buffered(1))
    return pl.pallas_call(
        functools.partial(_proj_ssd_kernel, tiles_per_batch=n_tiles // batch, n_heads=n_heads),
        grid=(n_tiles + 1,),
        in_specs=[
            pl.BlockSpec((tm, d), lambda s: (jnp.minimum(s, last), 0)),
            resident((1, d)),
            pl.BlockSpec((None, in_dim, d), lambda s: (layer, 0, 0), pipeline_mode=pl.Buffered(1)),
            resident((CONV_K, 1, n_conv)),
            resident((1, n_conv)),
            resident((1, LANES)),
            resident((1, LANES)),
            resident((1, n_inner)),
            resident((1, n_inner)),
        ],
        out_specs=[
            pl.BlockSpec((tm, nqkv), lambda s: (jnp.minimum(s, last), 0)),
            pl.BlockSpec((tm, n_inner), lambda s: (jnp.maximum(s - 1, 0), 0)),
        ],
        out_shape=[
            jax.ShapeDtypeStruct((t, nqkv), BF16),
            jax.ShapeDtypeStruct((t, n_inner), BF16),
        ],
        scratch_shapes=[
            pltpu.VMEM((nzx, d), BF16),
            pltpu.VMEM((ndt, d), BF16),
            pltpu.VMEM((nqkv, d), BF16),
            pltpu.VMEM((tm, nzx), F32),
            pltpu.VMEM((tm, ndt), F32),
            pltpu.VMEM((tm, nzx), F32),
            pltpu.VMEM((tm, ndt), F32),
            pltpu.VMEM((SUBLANES, n_conv), F32),
            pltpu.VMEM((SSD_GROUPS * SSD_STATE, n_inner), F32),
        ],
        compiler_params=_cparams(("arbitrary",)),
        name="proj_ssd",
    )(h, g, w_in, cw, cb, dtb, alog, dsk, gn)


def _sb_kernel(q_ref, k_ref, v_ref, u2_ref, u2h_ref, g_ref, o_ref, acc, rsum, z_scr, hl_scr, s_scr, w_scr):
    n_group = q_ref.shape[0]
    n_pairs = q_ref.shape[2] // LANES
    units = [(g, p) for g in range(n_group) for p in range(n_pairs)]
    n_units = len(units)

    row2 = lax.broadcasted_iota(jnp.int32, (SB_TQ, 2 * SB_TK), 0)
    col2 = lax.broadcasted_iota(jnp.int32, (SB_TQ, 2 * SB_TK), 1)
    below = (col2 & (SB_TK - 1)) < row2
    lane = lax.broadcasted_iota(jnp.int32, (SB_TK, LANES), 1)
    lo_m = jnp.where(lane < SB_HEAD_DIM, 1.0, 0.0).astype(BF16)
    hi_m = jnp.where(lane >= SB_HEAD_DIM, 1.0, 0.0).astype(BF16)
    lane_h = lax.broadcasted_iota(jnp.int32, (SB_HALF, LANES), 1)
    head_masks = {SB_TK: (lo_m, hi_m),
                  SB_HALF: (jnp.where(lane_h < SB_HEAD_DIM, 1.0, 0.0).astype(BF16),
                            jnp.where(lane_h >= SB_HEAD_DIM, 1.0, 0.0).astype(BF16))}

    def block_front(q_rows, off, width, diagonal, slot):
        base = slot * n_units
        w2 = 2 * width
        u_ref = u2_ref if width == SB_TK else u2h_ref
        for u, (g, p) in enumerate(units):
            cs = slice(p * LANES, (p + 1) * LANES)
            kb = k_ref[g, pl.ds(off, width), cs]
            k2 = jnp.concatenate([kb * head_masks[width][0], kb * head_masks[width][1]], axis=0)
            z = _dot_nt(q_ref[g, q_rows, cs], k2)
            z_scr[base + u, :, 0:w2] = z * LOG2_E
        for u in range(n_units):
            z2 = z_scr[base + u, :, 0:w2]
            sp2 = jnp.maximum(z2, 0.0) + jnp.log2(1.0 + jnp.exp2(-jnp.abs(z2)))
            if diagonal:
                sp2 = jnp.where(below, sp2, 0.0)
                z_scr[base + u, :, 0:w2] = jnp.where(below, z2, NEG_BIG)
            hi, lo = _split_bf16(sp2, 2)
            hl_scr[base + u, :, 0:w2] = hi
            hl_scr[base + u, :, w2:2 * w2] = lo
        for u in range(n_units):
            s_scr[base + u, :, 0:w2] = _dot(hl_scr[base + u, :, 0:2 * w2], u_ref[...])

    def block_back(off, width, slot):
        base = slot * n_units
        w2 = 2 * width
        rmax = None
        for u in range(n_units):
            if width == SB_TK:
                r_prev = jnp.concatenate([rsum[2 * u], rsum[2 * u + 1]], axis=1)
            else:
                r_prev = jnp.where(lane < width, rsum[2 * u], rsum[2 * u + 1])
            s = s_scr[base + u, :, 0:w2] + r_prev
            w_scr[base + u, :, 0:w2] = jnp.exp2(z_scr[base + u, :, 0:w2] + s).astype(BF16)
            for e in range(2):
                r_new = jnp.broadcast_to(s[:, e * width:e * width + 1], (SB_TQ, LANES))
                rsum[2 * u + e] = r_new
                rmax = r_new if rmax is None else jnp.maximum(rmax, r_new)
        for u, (g, p) in enumerate(units):
            cs = slice(p * LANES, (p + 1) * LANES)
            vb = v_ref[g, pl.ds(off, width), cs]
            v2 = jnp.concatenate([vb * head_masks[width][0], vb * head_masks[width][1]], axis=0)
            acc[u] += _dot(w_scr[base + u, :, 0:w2], v2)
        return jnp.max(rmax)

    go_on = lambda rmax: rmax > SB_SKIP_BELOW * LOG2_E

    def walk_full_blocks(q_rows, top, count, rmax_start):
        def cond(carry):
            i, rmax = carry
            return jnp.logical_and(i < count, go_on(rmax))

        def body(carry):
            i, _ = carry
            off = pl.multiple_of(top - (i + 1) * SB_TK, SB_HALF)
            block_front(q_rows, off, SB_TK, False, 0)
            return i + 1, block_back(off, SB_TK, 0)

        return lax.while_loop(cond, body, (jnp.int32(0), rmax_start))[1]

    def query_block(qb, carry):
        qi = pl.program_id(1) * SB_QB_PER_STEP + qb
        q_rows = pl.ds(pl.multiple_of(qb * SB_TQ, SB_TQ), SB_TQ)
        start = pl.multiple_of(qi * SB_TK, SB_TK)
        acc[...] = jnp.zeros(acc.shape, F32)
        rsum[...] = jnp.zeros(rsum.shape, F32)

        @pl.when(qi >= 2)
        def _():
            plan = [(start, SB_TK), (start - SB_TK, SB_TK), (start - SB_TK - SB_HALF, SB_HALF)]
            for b, (off, width) in enumerate(plan):
                block_front(q_rows, pl.multiple_of(off, SB_HALF), width, b == 0, b)
            for b, (off, width) in enumerate(plan):
                rmax = block_back(pl.multiple_of(off, SB_HALF), width, b)
            rmax = walk_full_blocks(q_rows, start - SB_TK - SB_HALF, qi - 2, rmax)

            @pl.when(go_on(rmax))
            def _():
                block_front(q_rows, 0, SB_HALF, False, 0)
                block_back(0, SB_HALF, 0)

        @pl.when(qi < 2)
        def _():
            block_front(q_rows, start, SB_TK, True, 0)
            walk_full_blocks(q_rows, start, qi, block_back(start, SB_TK, 0))

        for g in range(n_group):
            o = jnp.concatenate([acc[g * n_pairs + p] for p in range(n_pairs)], axis=1)
            o_ref[g, q_rows, :] = _rms(o, g_ref[...]).astype(BF16)
        return carry

    lax.fori_loop(0, SB_QB_PER_STEP, query_block, 0)


def _sb_attention(qkv, u2, u2h, g, batch):
    t, n3 = qkv.shape
    n = n3 // 3
    s = t // batch
    tq = SB_QB_PER_STEP * SB_TQ
    nq = s // tq
    n_pairs = n // LANES
    group = SB_GROUP if batch % SB_GROUP == 0 else 1
    n_units = group * n_pairs
    qkv3 = qkv.reshape(batch, s, n3)
    out = pl.pallas_call(
        _sb_kernel,
        grid=(batch // group, nq),
        in_specs=[
            pl.BlockSpec((group, tq, n), lambda b, i: (b, i, 0)),
            pl.BlockSpec((group, s, n), lambda b, i: (b, 0, 1)),
            pl.BlockSpec((group, s, n), lambda b, i: (b, 0, 2)),
            pl.BlockSpec((4 * SB_TK, 2 * SB_TK), lambda b, i: (0, 0)),
            pl.BlockSpec((4 * SB_HALF, 2 * SB_HALF), lambda b, i: (0, 0)),
            pl.BlockSpec((1, n), lambda b, i: (0, 0)),
        ],
        out_specs=pl.BlockSpec((group, tq, n), lambda b, i: (b, i, 0)),
        out_shape=jax.ShapeDtypeStruct((batch, s, n), BF16),
        scratch_shapes=[
            pltpu.VMEM((n_units, SB_TQ, LANES), F32),
            pltpu.VMEM((2 * n_units, SB_TQ, LANES), F32),
            pltpu.VMEM((SB_STATIC_BLOCKS * n_units, SB_TQ, 2 * SB_TK), F32),
            pltpu.VMEM((SB_STATIC_BLOCKS * n_units, SB_TQ, 4 * SB_TK), BF16),
            pltpu.VMEM((SB_STATIC_BLOCKS * n_units, SB_TQ, 2 * SB_TK), F32),
            pltpu.VMEM((SB_STATIC_BLOCKS * n_units, SB_TQ, 2 * SB_TK), BF16),
        ],
        compiler_params=_cparams(("parallel", "arbitrary")),
        name="sb_attn",
    )(qkv3, qkv3, qkv3, u2, u2h, g)
    return out.reshape(t, n)


def _memkv_kernel(m_ref, g_ref, wk_ref, wv_ref, k_ref, v_ref):
    mn = _rms(m_ref[...], g_ref[...]).astype(BF16)
    k_ref[...] = _dot(mn, wk_ref[...]).astype(BF16)
    v_ref[...] = _dot(mn, wv_ref[...]).astype(BF16)


def _mem_kv(mem2d, g, wk, wv, layer, batch):
    tmem, d = mem2d.shape
    m = tmem // batch
    n = wk.shape[2]
    const = lambda b: (0, 0)
    row = lambda b: (b, 0)
    return pl.pallas_call(
        _memkv_kernel,
        grid=(batch,),
        in_specs=[
            pl.BlockSpec((m, d), row),
            pl.BlockSpec((1, d), const),
            pl.BlockSpec((None, d, n), lambda b: (layer, 0, 0)),
            pl.BlockSpec((None, d, n), lambda b: (layer, 0, 0)),
        ],
        out_specs=[pl.BlockSpec((m, n), row), pl.BlockSpec((m, n), row)],
        out_shape=[jax.ShapeDtypeStruct((tmem, n), BF16)] * 2,
        compiler_params=_cparams(("parallel",)),
        name="mem_kv",
    )(mem2d, g, wk, wv)


def _tail_kernel(h_ref, ya_ref, yb_ref, wa_ref, wb_ref, gx_ref, wq_ref, k_ref, v_ref, wo_ref, gf_ref, w1_ref, w2_ref,
                 gfin_ref, o_ref, *, ff_chunk, final_norm):
    h = h_ref[...] + _dot(ya_ref[...], wa_ref[...]) + _dot(yb_ref[...], wb_ref[...])

    hn = _rms(h, gx_ref[...]).astype(BF16)
    q = _dot(hn, wq_ref[...]).astype(BF16)
    scale = 1.0 / math.sqrt(XA_HEAD_DIM)
    outs = []
    for hd in range(q.shape[1] // XA_HEAD_DIM):
        cs = slice(hd * XA_HEAD_DIM, (hd + 1) * XA_HEAD_DIM)
        logits = _dot_nt(q[:, cs], k_ref[:, cs]) * scale
        e = jnp.exp(logits - jnp.max(logits, axis=-1, keepdims=True))
        denom = jnp.sum(e, axis=-1, keepdims=True)
        outs.append(_dot(e.astype(BF16), v_ref[:, cs]) / denom)
    h = h + _dot(jnp.concatenate(outs, axis=1).astype(BF16), wo_ref[...])

    hn = _rms(h, gf_ref[...]).astype(BF16)
    out = h
    for c in range(w1_ref.shape[1] // ff_chunk):
        cs = slice(c * ff_chunk, (c + 1) * ff_chunk)
        u = jnp.maximum(_dot(hn, w1_ref[:, cs]), 0.0)
        out = out + _dot((u * u).astype(BF16), w2_ref[cs, :])
    if final_norm:
        out = _rms(out, gfin_ref[...])
    o_ref[...] = out


def _layer_tail(h, ya, yb, w_out, gx, wq, k, v, wo, gf, w1, w2, gfin, layer, batch, final_norm, tm=1024, ff_chunk=1024):
    t, d = h.shape
    na, nb = ya.shape[1], yb.shape[1]
    assert na == nb
    n = wq.shape[2]
    f = w1.shape[2]
    m = k.shape[0] // batch
    tiles_per_batch = t // batch // tm
    row = lambda i: (i, 0)
    kv = lambda i: (i // tiles_per_batch, 0)
    resident = lambda shape: pl.BlockSpec(shape, lambda i: (0, 0), pipeline_mode=pl.Buffered(1))
    weight = lambda shape, r=0: pl.BlockSpec((None,) + shape, lambda i: (layer, r, 0), pipeline_mode=pl.Buffered(1))
    return pl.pallas_call(
        functools.partial(_tail_kernel, ff_chunk=ff_chunk, final_norm=final_norm),
        grid=(t // tm,),
        in_specs=[
            pl.BlockSpec((tm, d), row),
            pl.BlockSpec((tm, na), row),
            pl.BlockSpec((tm, nb), row),
            weight((na, d), 0),
            weight((nb, d), 1),
            resident((1, d)),
            weight((d, n)),
            pl.BlockSpec((m, n), kv),
            pl.BlockSpec((m, n), kv),
            weight((n, d)),
            resident((1, d)),
            weight((d, f)),
            weight((f, d)),
            resident((1, d)),
        ],
        out_specs=pl.BlockSpec((tm, d), row),
        out_shape=jax.ShapeDtypeStruct((t, d), F32),
        compiler_params=_cparams(("parallel",)),
        name="layer_tail",
    )(h, ya, yb, w_out, w_out, gx, wq, k, v, wo, gf, w1, w2, gfin)


def _pad_lanes(v):
    return jnp.pad(v.astype(F32), (0, LANES - v.shape[0])).reshape(1, LANES)


def kernel(x, mem, norm_mix_g, w_in, conv_w, conv_b, dt_bias, a_log, d_skip, ssd_norm_g, sb_norm_g, w_out,
           norm_xa_g, norm_mem_g, w_xq, w_xk, w_xv, w_xo, norm_ff_g, w_ff1, w_ff2, final_g):
    batch, seq, d = x.shape
    depth = w_in.shape[0]
    n_heads = dt_bias.shape[1]
    t = batch * seq
    row = lambda v: v.astype(F32).reshape(1, -1)

    def later_key_sums(width):
        jj = lax.broadcasted_iota(jnp.int32, (2 * width, 2 * width), 0)
        ss = lax.broadcasted_iota(jnp.int32, (2 * width, 2 * width), 1)
        u = -((jj // width == ss // width) & (jj >= ss)).astype(BF16)
        return jnp.concatenate([u, u], axis=0)

    u2, u2h = later_key_sums(SB_TK), later_key_sums(SB_HALF)

    w_out_b, w_xq_b, w_xk_b, w_xv_b, w_xo_b, w_ff1_b, w_ff2_b = (
        w.astype(BF16) for w in (w_out, w_xq, w_xk, w_xv, w_xo, w_ff1, w_ff2))
    w_in = jnp.swapaxes(w_in.astype(F32), 1, 2)

    h = x.reshape(t, d)
    mem2d = mem.reshape(batch * mem.shape[1], d)
    for l in range(depth):
        qkv, y_ssd = _proj_ssd(h, row(norm_mix_g[l]), w_in, l, conv_w[l].astype(F32)[:, None, :],
                               row(conv_b[l]), _pad_lanes(dt_bias[l]), _pad_lanes(a_log[l]),
                               row(jnp.repeat(d_skip[l], SSD_HEAD_DIM)), row(ssd_norm_g[l]), batch, n_heads)
        y_sb = _sb_attention(qkv, u2, u2h, row(sb_norm_g[l]), batch)
        k_mem, v_mem = _mem_kv(mem2d, row(norm_mem_g[l]), w_xk_b, w_xv_b, l, batch)
        h = _layer_tail(h, y_ssd, y_sb, w_out_b, row(norm_xa_g[l]), w_xq_b, k_mem, v_mem, w_xo_b,
                        row(norm_ff_g[l]), w_ff1_b, w_ff2_b, row(final_g), l, batch, final_norm=(l == depth - 1))
    return h.reshape(batch, seq, d)
```

```python
import functools
import math

import jax
import jax.numpy as jnp
from jax import lax
from jax.experimental import pallas as pl
from jax.experimental.pallas import tpu as pltpu

F32 = jnp.float32
BF16 = jnp.bfloat16

EPS = 1e-5
CONV_K = 4
CHUNK = 128
SSD_HEAD_DIM = 64
SSD_GROUPS = 2
SSD_STATE = 64
SB_HEAD_DIM = 64
XA_HEAD_DIM = 128

LANES = 128
SUBLANES = 8
MXU_N = 256
VMEM_LIMIT = 56 * 1024 * 1024

SB_TQ = 128
SB_TK = 128
SB_GROUP = 2
SB_STATIC_BLOCKS = 3
SB_QB_PER_STEP = 4
SB_SKIP_BELOW = -110.0
NEG_BIG = -1e30
LOG2_E = math.log2(math.e)


def _cparams(sem):
    return pltpu.CompilerParams(dimension_semantics=sem, vmem_limit_bytes=VMEM_LIMIT)


def _rms(x, g):
    return x * lax.rsqrt(jnp.mean(x * x, axis=-1, keepdims=True) + EPS) * g


def _softplus(x):
    return jnp.maximum(x, 0.0) + jnp.log(1.0 + jnp.exp(-jnp.abs(x)))


def _dot(a, b):
    return jnp.dot(a, b, preferred_element_type=F32)


def _dot_nt(a, b):
    return lax.dot_general(a, b, (((1,), (1,)), ((), ())), preferred_element_type=F32)


def _split_bf16(x, parts):
    out = []
    r = x
    for _ in range(parts - 1):
        p = r.astype(BF16)
        out.append(p)
        r = r - p.astype(F32)
    out.append(r.astype(BF16))
    return out


def _expand_heads(a, lane):
    cols = []
    for j in range(4):
        lo = jnp.broadcast_to(a[:, 2 * j:2 * j + 1], (a.shape[0], LANES))
        hi = jnp.broadcast_to(a[:, 2 * j + 1:2 * j + 2], (a.shape[0], LANES))
        cols.append(jnp.where(lane < SSD_HEAD_DIM, lo, hi))
    return jnp.concatenate(cols, axis=1)


def _ssd_chunk(z, window, dt_raw, st, cw_ref, cb_ref, dtb_ref, alog_ref, dsk_ref, gn_ref, interleave):
    n_inner = z.shape[1]
    halo = window.shape[0] - CHUNK
    interleave = list(interleave)
    per_stage = -(-len(interleave) // 3)

    def run_interleaved(stage):
        for job in interleave[stage * per_stage:(stage + 1) * per_stage]:
            job()

    run_interleaved(0)

    conv = cb_ref[...] + cw_ref[CONV_K - 1] * window[halo:, :]
    for k in range(CONV_K - 1):
        shifted = pltpu.roll(window, CONV_K - 1 - k, 0)[halo:, :]
        conv = conv + cw_ref[k] * shifted
    xbc = conv * jax.nn.sigmoid(conv)
    xs = xbc[:, :n_inner]
    bm = xbc[:, n_inner:n_inner + LANES]
    cm = xbc[:, n_inner + LANES:]

    row = lax.broadcasted_iota(jnp.int32, (CHUNK, CHUNK), 0)
    col = lax.broadcasted_iota(jnp.int32, (CHUNK, CHUNK), 1)
    causal = row >= col
    lane = col

    dt = _softplus(dt_raw + dtb_ref[...])
    a_c = dt * (-jnp.exp(alog_ref[...]))
    tri = jnp.where(causal, 1.0, 0.0).astype(BF16)
    a_cum = None
    for part in _split_bf16(a_c, 3):
        term = _dot(tri, part)
        a_cum = term if a_cum is None else a_cum + term
    a_cum_t = a_cum.T

    dt_x = _expand_heads(dt, lane)
    acum_x = _expand_heads(a_cum, lane)
    last_x = acum_x[CHUNK - 1:CHUNK, :]
    xc = xs * dt_x
    e_acum = jnp.exp(acum_x)
    xd = (xc * jnp.exp(last_x - acum_x)).astype(BF16)
    chunk_decay = jnp.exp(last_x)
    run_interleaved(1)

    bm_b = bm.astype(BF16)
    cbs = []
    for g in range(SSD_GROUPS):
        in_g = (lane >= g * SSD_STATE) & (lane < (g + 1) * SSD_STATE)
        cbs.append(_dot_nt(jnp.where(in_g, cm, 0.0).astype(BF16), bm_b))
    y_cols = []
    for j in range(n_inner // LANES):
        ms = []
        for h in (2 * j, 2 * j + 1):
            seg = a_cum[:, h:h + 1] - a_cum_t[h:h + 1, :]
            decay = jnp.exp(jnp.where(causal, seg, NEG_BIG))
            ms.append((cbs[h * SSD_HEAD_DIM * SSD_GROUPS // n_inner] * decay).astype(BF16))
        xcj = xc[:, j * LANES:(j + 1) * LANES]
        x_lo = jnp.where(lane < SSD_HEAD_DIM, xcj, 0.0).astype(BF16)
        x_hi = jnp.where(lane >= SSD_HEAD_DIM, xcj, 0.0).astype(BF16)
        y_cols.append(_dot(jnp.concatenate(ms, axis=1), jnp.concatenate([x_lo, x_hi], axis=0)))
    y = jnp.concatenate(y_cols, axis=1)
    run_interleaved(2)

    y = y + _dot(cm.astype(BF16), st.astype(BF16)) * e_acum
    srow = lax.broadcasted_iota(jnp.int32, st.shape, 0)
    scol = lax.broadcasted_iota(jnp.int32, st.shape, 1)
    same_group = (srow // SSD_STATE) == (scol // (n_inner // SSD_GROUPS))
    new_st = st * chunk_decay + jnp.where(same_group, _dot(bm.T.astype(BF16), xd), 0.0)

    y = y + dsk_ref[...] * xs
    y = y * (z * jax.nn.sigmoid(z))
    return _rms(y, gn_ref[...]).astype(BF16), new_st


def _proj_ssd_kernel(h_ref, g_ref, win_ref, cw_ref, cb_ref, dtb_ref, alog_ref, dsk_ref, gn_ref,
                     qkv_ref, y_ref, wzx_ref, wdt_ref, wqkv_ref, zx_a, dt_a, zx_b, dt_b, halo_scr, state,
                     *, tiles_per_batch, n_heads):
    s = pl.program_id(0)
    tm = h_ref.shape[0]
    n_inner = y_ref.shape[1]

    @pl.when(s == 0)
    def _():
        zx_b[...] = jnp.zeros(zx_b.shape, F32)
        dt_b[...] = jnp.zeros(dt_b.shape, F32)
        halo_scr[...] = jnp.zeros(halo_scr.shape, F32)
        state[...] = jnp.zeros(state.shape, F32)
        nzx, nqkv = wzx_ref.shape[0], wqkv_ref.shape[0]
        n_sb = nqkv // 3
        q_scale = 1.0 / math.sqrt(SB_HEAD_DIM)
        for r in range(0, nzx, CHUNK):
            wzx_ref[r:r + CHUNK, :] = win_ref[r:r + CHUNK, :].astype(BF16)
        row = lax.broadcasted_iota(jnp.int32, wdt_ref.shape, 0)
        wdt_ref[...] = jnp.where(row < n_heads, win_ref[nzx:nzx + wdt_ref.shape[0], :], 0.0).astype(BF16)
        for r in range(0, nqkv, CHUNK):
            blk = win_ref[nzx + n_heads + r:nzx + n_heads + r + CHUNK, :]
            wqkv_ref[r:r + CHUNK, :] = (blk * q_scale if r < n_sb else blk).astype(BF16)

    def step(zx_w, dt_w, zx_r, dt_r):
        hn = _rms(h_ref[...], g_ref[...]).astype(BF16)
        n_chunks = tm // CHUNK
        nzx, nqkv = zx_w.shape[1], qkv_ref.shape[1]

        def project(out_ref, w_ref, a):
            b = min(a + MXU_N, w_ref.shape[0])
            out_ref[:, a:b] = _dot_nt(hn, w_ref[a:b, :]).astype(out_ref.dtype)

        jobs = [functools.partial(project, zx_w, wzx_ref, a) for a in range(0, nzx, MXU_N)]
        jobs += [functools.partial(project, dt_w, wdt_ref, 0)]
        jobs += [functools.partial(project, qkv_ref, wqkv_ref, a) for a in range(0, nqkv, MXU_N)]
        share = -(-len(jobs) // n_chunks)

        fresh = lax.rem(s - 1, tiles_per_batch) == 0
        st = jnp.where(fresh, 0.0, state[...])
        halo = jnp.where(fresh, 0.0, halo_scr[...])
        for c in range(n_chunks):
            r0 = c * CHUNK
            if c == 0:
                window = jnp.concatenate([halo, zx_r[0:CHUNK, n_inner:]], axis=0)
            else:
                window = zx_r[r0 - SUBLANES:r0 + CHUNK, n_inner:]
            y, st = _ssd_chunk(zx_r[r0:r0 + CHUNK, :n_inner], window, dt_r[r0:r0 + CHUNK, :], st,
                               cw_ref, cb_ref, dtb_ref, alog_ref, dsk_ref, gn_ref, jobs[c * share:(c + 1) * share])
            y_ref[r0:r0 + CHUNK, :] = y
        state[...] = st
        halo_scr[...] = zx_r[tm - SUBLANES:tm, n_inner:]

    @pl.when(s % 2 == 0)
    def _():
        step(zx_a, dt_a, zx_b, dt_b)

    @pl.when(s % 2 == 1)
    def _():
        step(zx_b, dt_b, zx_a, dt_a)


def _proj_ssd(h, g, w_in, layer, cw, cb, dtb, alog, dsk, gn, batch, n_heads, tm=512):
    t, d = h.shape
    n_inner = gn.shape[2]
    n_conv = cb.shape[2]
    nzx, ndt = n_inner + n_conv, LANES
    in_dim = w_in.shape[1]
    nqkv = in_dim - nzx - n_heads
    n_tiles = t // tm
    last = n_tiles - 1
    of_layer = lambda *shape: pl.BlockSpec((None,) + shape, lambda s: (layer,) + (0,) * len(shape),
                                           pipeline_mode=pl.Buffered(1))
    return pl.pallas_call(
        functools.partial(_proj_ssd_kernel, tiles_per_batch=n_tiles // batch, n_heads=n_heads),
        grid=(n_tiles + 1,),
        in_specs=[
            pl.BlockSpec((tm, d), lambda s: (jnp.minimum(s, last), 0)),
            of_layer(1, d),
            of_layer(in_dim, d),
            of_layer(CONV_K, 1, n_conv),
            of_layer(1, n_conv),
            of_layer(1, LANES),
            of_layer(1, LANES),
            of_layer(1, n_inner),
            of_layer(1, n_inner),
        ],
        out_specs=[
            pl.BlockSpec((tm, nqkv), lambda s: (jnp.minimum(s, last), 0)),
            pl.BlockSpec((tm, n_inner), lambda s: (jnp.maximum(s - 1, 0), 0)),
        ],
        out_shape=[
            jax.ShapeDtypeStruct((t, nqkv), BF16),
            jax.ShapeDtypeStruct((t, n_inner), BF16),
        ],
        scratch_shapes=[
            pltpu.VMEM((nzx, d), BF16),
            pltpu.VMEM((ndt, d), BF16),
            pltpu.VMEM((nqkv, d), BF16),
            pltpu.VMEM((tm, nzx), F32),
            pltpu.VMEM((tm, ndt), F32),
            pltpu.VMEM((tm, nzx), F32),
            pltpu.VMEM((tm, ndt), F32),
            pltpu.VMEM((SUBLANES, n_conv), F32),
            pltpu.VMEM((SSD_GROUPS * SSD_STATE, n_inner), F32),
        ],
        compiler_params=_cparams(("arbitrary",)),
        name="proj_ssd",
    )(h, g, w_in, cw, cb, dtb, alog, dsk, gn)


def _sb_kernel(q_ref, k_ref, v_ref, u2_ref, g_ref, o_ref, acc, rsum, z_scr, hl_scr, s_scr, w_scr):
    n_group = q_ref.shape[0]
    n_pairs = q_ref.shape[2] // LANES
    units = [(g, p) for g in range(n_group) for p in range(n_pairs)]
    n_units = len(units)
    n_ahead = SB_STATIC_BLOCKS - 1

    row2 = lax.broadcasted_iota(jnp.int32, (SB_TQ, 2 * SB_TK), 0)
    col2 = lax.broadcasted_iota(jnp.int32, (SB_TQ, 2 * SB_TK), 1)
    below = (col2 & (SB_TK - 1)) < row2
    lane = lax.broadcasted_iota(jnp.int32, (SB_TK, LANES), 1)
    lo_m = jnp.where(lane < SB_HEAD_DIM, 1.0, 0.0).astype(BF16)
    hi_m = jnp.where(lane >= SB_HEAD_DIM, 1.0, 0.0).astype(BF16)

    def block_front(q_rows, j, diagonal, slot):
        off = pl.multiple_of(j * SB_TK, SB_TK)
        base = slot * n_units
        for u, (g, p) in enumerate(units):
            cs = slice(p * LANES, (p + 1) * LANES)
            kb = k_ref[g, pl.ds(off, SB_TK), cs]
            k2 = jnp.concatenate([kb * lo_m, kb * hi_m], axis=0)
            z = _dot_nt(q_ref[g, q_rows, cs], k2)
            z_scr[base + u] = z * LOG2_E
        for u in range(n_units):
            z2 = z_scr[base + u]
            sp2 = jnp.maximum(z2, 0.0) + jnp.log2(1.0 + jnp.exp2(-jnp.abs(z2)))
            if diagonal:
                sp2 = jnp.where(below, sp2, 0.0)
                z_scr[base + u] = jnp.where(below, z2, NEG_BIG)
            hi, lo = _split_bf16(sp2, 2)
            hl_scr[base + u, :, 0:2 * SB_TK] = hi
            hl_scr[base + u, :, 2 * SB_TK:4 * SB_TK] = lo
        for u in range(n_units):
            s_scr[base + u] = _dot(hl_scr[base + u], u2_ref[...])

    def block_back(j, slot):
        off = pl.multiple_of(j * SB_TK, SB_TK)
        base = slot * n_units
        rmax = None
        for u in range(n_units):
            r_prev = jnp.concatenate([rsum[2 * u], rsum[2 * u + 1]], axis=1)
            s = s_scr[base + u] + r_prev
            w_scr[base + u] = jnp.exp2(z_scr[base + u] + s).astype(BF16)
            for e in range(2):
                r_new = jnp.broadcast_to(s[:, e * SB_TK:e * SB_TK + 1], (SB_TQ, LANES))
                rsum[2 * u + e] = r_new
                rmax = r_new if rmax is None else jnp.maximum(rmax, r_new)
        for u, (g, p) in enumerate(units):
            cs = slice(p * LANES, (p + 1) * LANES)
            vb = v_ref[g, pl.ds(off, SB_TK), cs]
            v2 = jnp.concatenate([vb * lo_m, vb * hi_m], axis=0)
            acc[u] += _dot(w_scr[base + u], v2)
        return jnp.max(rmax)

    def walk_rest(q_rows, j_start, rmax_start):
        def cond(carry):
            j, rmax = carry
            return jnp.logical_and(j >= 0, rmax > SB_SKIP_BELOW * LOG2_E)

        def body(carry):
            j, _ = carry
            block_front(q_rows, j, False, 0)
            return j - 1, block_back(j, 0)

        lax.while_loop(cond, body, (j_start, rmax_start))

    def query_block(qb, carry):
        qi = pl.program_id(1) * SB_QB_PER_STEP + qb
        q_rows = pl.ds(pl.multiple_of(qb * SB_TQ, SB_TQ), SB_TQ)
        acc[...] = jnp.zeros(acc.shape, F32)
        rsum[...] = jnp.zeros(rsum.shape, F32)

        @pl.when(qi >= n_ahead)
        def _():
            for b in range(SB_STATIC_BLOCKS):
                block_front(q_rows, qi - b, b == 0, b)
            for b in range(SB_STATIC_BLOCKS):
                rmax = block_back(qi - b, b)
            walk_rest(q_rows, qi - SB_STATIC_BLOCKS, rmax)

        @pl.when(qi < n_ahead)
        def _():
            block_front(q_rows, qi, True, 0)
            walk_rest(q_rows, qi - 1, block_back(qi, 0))

        for g in range(n_group):
            o = jnp.concatenate([acc[g * n_pairs + p] for p in range(n_pairs)], axis=1)
            o_ref[g, q_rows, :] = _rms(o, g_ref[...]).astype(BF16)
        return carry

    lax.fori_loop(0, SB_QB_PER_STEP, query_block, 0)


def _sb_attention(qkv, u2, g, layer, batch):
    t, n3 = qkv.shape
    n = n3 // 3
    s = t // batch
    tq = SB_QB_PER_STEP * SB_TQ
    nq = s // tq
    n_pairs = n // LANES
    group = SB_GROUP if batch % SB_GROUP == 0 else 1
    n_units = group * n_pairs
    qkv3 = qkv.reshape(batch, s, n3)
    out = pl.pallas_call(
        _sb_kernel,
        grid=(batch // group, nq),
        in_specs=[
            pl.BlockSpec((group, tq, n), lambda b, i: (b, i, 0)),
            pl.BlockSpec((group, s, n), lambda b, i: (b, 0, 1)),
            pl.BlockSpec((group, s, n), lambda b, i: (b, 0, 2)),
            pl.BlockSpec((4 * SB_TK, 2 * SB_TK), lambda b, i: (0, 0)),
            pl.BlockSpec((None, 1, n), lambda b, i: (layer, 0, 0)),
        ],
        out_specs=pl.BlockSpec((group, tq, n), lambda b, i: (b, i, 0)),
        out_shape=jax.ShapeDtypeStruct((batch, s, n), BF16),
        scratch_shapes=[
            pltpu.VMEM((n_units, SB_TQ, LANES), F32),
            pltpu.VMEM((2 * n_units, SB_TQ, LANES), F32),
            pltpu.VMEM((SB_STATIC_BLOCKS * n_units, SB_TQ, 2 * SB_TK), F32),
            pltpu.VMEM((SB_STATIC_BLOCKS * n_units, SB_TQ, 4 * SB_TK), BF16),
            pltpu.VMEM((SB_STATIC_BLOCKS * n_units, SB_TQ, 2 * SB_TK), F32),
            pltpu.VMEM((SB_STATIC_BLOCKS * n_units, SB_TQ, 2 * SB_TK), BF16),
        ],
        compiler_params=_cparams(("parallel", "arbitrary")),
        name="sb_attn",
    )(qkv3, qkv3, qkv3, u2, g)
    return out.reshape(t, n)


def _memkv_kernel(m_ref, g_ref, wk_ref, wv_ref, k_ref, v_ref):
    mn = _rms(m_ref[...], g_ref[...]).astype(BF16)
    k_ref[...] = _dot(mn, wk_ref[...]).astype(BF16)
    v_ref[...] = _dot(mn, wv_ref[...]).astype(BF16)


def _mem_kv(mem2d, g, wk, wv, layer, batch):
    tmem, d = mem2d.shape
    m = tmem // batch
    n = wk.shape[2]
    row = lambda b: (b, 0)
    return pl.pallas_call(
        _memkv_kernel,
        grid=(batch,),
        in_specs=[
            pl.BlockSpec((m, d), row),
            pl.BlockSpec((None, 1, d), lambda b: (layer, 0, 0)),
            pl.BlockSpec((None, d, n), lambda b: (layer, 0, 0)),
            pl.BlockSpec((None, d, n), lambda b: (layer, 0, 0)),
        ],
        out_specs=[pl.BlockSpec((m, n), row), pl.BlockSpec((m, n), row)],
        out_shape=[jax.ShapeDtypeStruct((tmem, n), BF16)] * 2,
        compiler_params=_cparams(("parallel",)),
        name="mem_kv",
    )(mem2d, g, wk, wv)


def _tail_kernel(h_ref, ya_ref, yb_ref, wa_ref, wb_ref, gx_ref, wq_ref, k_ref, v_ref, wo_ref, gf_ref, w1_ref, w2_ref,
                 gfin_ref, o_ref, *, ff_chunk, final_norm):
    h = h_ref[...] + _dot(ya_ref[...], wa_ref[...]) + _dot(yb_ref[...], wb_ref[...])

    hn = _rms(h, gx_ref[...]).astype(BF16)
    q = _dot(hn, wq_ref[...]).astype(BF16)
    scale = 1.0 / math.sqrt(XA_HEAD_DIM)
    outs = []
    for hd in range(q.shape[1] // XA_HEAD_DIM):
        cs = slice(hd * XA_HEAD_DIM, (hd + 1) * XA_HEAD_DIM)
        logits = _dot_nt(q[:, cs], k_ref[:, cs]) * scale
        e = jnp.exp(logits - jnp.max(logits, axis=-1, keepdims=True))
        denom = jnp.sum(e, axis=-1, keepdims=True)
        outs.append(_dot(e.astype(BF16), v_ref[:, cs]) / denom)
    h = h + _dot(jnp.concatenate(outs, axis=1).astype(BF16), wo_ref[...])

    hn = _rms(h, gf_ref[...]).astype(BF16)
    out = h
    for c in range(w1_ref.shape[1] // ff_chunk):
        cs = slice(c * ff_chunk, (c + 1) * ff_chunk)
        u = jnp.maximum(_dot(hn, w1_ref[:, cs]), 0.0)
        out = out + _dot((u * u).astype(BF16), w2_ref[cs, :])
    if final_norm:
        out = _rms(out, gfin_ref[...])
    o_ref[...] = out


def _layer_tail(h, ya, yb, w_out, gx, wq, k, v, wo, gf, w1, w2, gfin, layer, batch, final_norm, tm=1024, ff_chunk=1024):
    t, d = h.shape
    na, nb = ya.shape[1], yb.shape[1]
    assert na == nb
    n = wq.shape[2]
    f = w1.shape[2]
    m = k.shape[0] // batch
    tiles_per_batch = t // batch // tm
    row = lambda i: (i, 0)
    kv = lambda i: (i // tiles_per_batch, 0)
    resident = lambda shape: pl.BlockSpec(shape, lambda i: (0, 0), pipeline_mode=pl.Buffered(1))
    weight = lambda shape, r=0: pl.BlockSpec((None,) + shape, lambda i: (layer, r, 0), pipeline_mode=pl.Buffered(1))
    return pl.pallas_call(
        functools.partial(_tail_kernel, ff_chunk=ff_chunk, final_norm=final_norm),
        grid=(t // tm,),
        in_specs=[
            pl.BlockSpec((tm, d), row),
            pl.BlockSpec((tm, na), row),
            pl.BlockSpec((tm, nb), row),
            weight((na, d), 0),
            weight((nb, d), 1),
            weight((1, d)),
            weight((d, n)),
            pl.BlockSpec((m, n), kv),
            pl.BlockSpec((m, n), kv),
            weight((n, d)),
            weight((1, d)),
            weight((d, f)),
            weight((f, d)),
            resident((1, d)),
        ],
        out_specs=pl.BlockSpec((tm, d), row),
        out_shape=jax.ShapeDtypeStruct((t, d), F32),
        compiler_params=_cparams(("parallel",)),
        name="layer_tail",
    )(h, ya, yb, w_out, w_out, gx, wq, k, v, wo, gf, w1, w2, gfin)


def kernel(x, mem, norm_mix_g, w_in, conv_w, conv_b, dt_bias, a_log, d_skip, ssd_norm_g, sb_norm_g, w_out,
           norm_xa_g, norm_mem_g, w_xq, w_xk, w_xv, w_xo, norm_ff_g, w_ff1, w_ff2, final_g):
    batch, seq, d = x.shape
    depth = w_in.shape[0]
    n_heads = dt_bias.shape[1]
    t = batch * seq
    rows = lambda v: v.astype(F32)[:, None, :]

    jj = lax.broadcasted_iota(jnp.int32, (2 * SB_TK, 2 * SB_TK), 0)
    ss = lax.broadcasted_iota(jnp.int32, (2 * SB_TK, 2 * SB_TK), 1)
    u2 = -((jj // SB_TK == ss // SB_TK) & (jj >= ss)).astype(BF16)
    u2 = jnp.concatenate([u2, u2], axis=0)

    w_out_b, w_xq_b, w_xk_b, w_xv_b, w_xo_b, w_ff1_b, w_ff2_b = (
        w.astype(BF16) for w in (w_out, w_xq, w_xk, w_xv, w_xo, w_ff1, w_ff2))
    w_in = jnp.swapaxes(w_in.astype(F32), 1, 2)

    pad_heads = lambda v: jnp.pad(v.astype(F32), ((0, 0), (0, LANES - n_heads)))
    g_mix, g_ssd, g_sb, g_mem, g_xa, g_ff = (
        rows(v) for v in (norm_mix_g, ssd_norm_g, sb_norm_g, norm_mem_g, norm_xa_g, norm_ff_g))
    conv_taps = conv_w.astype(F32)[:, :, None, :]
    conv_bias = rows(conv_b)
    dt_b, a_lg = rows(pad_heads(dt_bias)), rows(pad_heads(a_log))
    skip = rows(jnp.repeat(d_skip, SSD_HEAD_DIM, axis=1))
    g_final = final_g.astype(F32).reshape(1, -1)

    h = x.reshape(t, d)
    mem2d = mem.reshape(batch * mem.shape[1], d)
    for l in range(depth):
        qkv, y_ssd = _proj_ssd(h, g_mix, w_in, l, conv_taps, conv_bias, dt_b, a_lg, skip, g_ssd, batch, n_heads)
        y_sb = _sb_attention(qkv, u2, g_sb, l, batch)
        k_mem, v_mem = _mem_kv(mem2d, g_mem, w_xk_b, w_xv_b, l, batch)
        h = _layer_tail(h, y_ssd, y_sb, w_out_b, g_xa, w_xq_b, k_mem, v_mem, w_xo_b, g_ff, w_ff1_b, w_ff2_b,
                        g_final, l, batch, final_norm=(l == depth - 1))
    return h.reshape(batch, seq, d)
```

```python
import functools
import math

import jax
import jax.numpy as jnp
from jax import lax
from jax.experimental import pallas as pl
from jax.experimental.pallas import tpu as pltpu

F32 = jnp.float32
BF16 = jnp.bfloat16

EPS = 1e-5
CONV_K = 4
CHUNK = 128
SSD_HEAD_DIM = 64
SSD_GROUPS = 2
SSD_STATE = 64
SB_HEAD_DIM = 64
XA_HEAD_DIM = 128

LANES = 128
SUBLANES = 8
MXU_N = 256
VMEM_LIMIT = 56 * 1024 * 1024

SB_TQ = 128
SB_TK = 128
SB_GROUP = 2
SB_STATIC_BLOCKS = 3
SB_QB_PER_STEP = 4
SB_SKIP_BELOW = -110.0
NEG_BIG = -1e30
LOG2_E = math.log2(math.e)


def _cparams(sem):
    return pltpu.CompilerParams(dimension_semantics=sem, vmem_limit_bytes=VMEM_LIMIT)


def _rms(x, g):
    return x * lax.rsqrt(jnp.mean(x * x, axis=-1, keepdims=True) + EPS) * g


def _softplus(x):
    return jnp.maximum(x, 0.0) + jnp.log(1.0 + jnp.exp(-jnp.abs(x)))


def _dot(a, b):
    return jnp.dot(a, b, preferred_element_type=F32)


def _dot_nt(a, b):
    return lax.dot_general(a, b, (((1,), (1,)), ((), ())), preferred_element_type=F32)


def _split_bf16(x, parts):
    out = []
    r = x
    for _ in range(parts - 1):
        p = r.astype(BF16)
        out.append(p)
        r = r - p.astype(F32)
    out.append(r.astype(BF16))
    return out


def _expand_heads(a, lane):
    cols = []
    for j in range(4):
        lo = jnp.broadcast_to(a[:, 2 * j:2 * j + 1], (a.shape[0], LANES))
        hi = jnp.broadcast_to(a[:, 2 * j + 1:2 * j + 2], (a.shape[0], LANES))
        cols.append(jnp.where(lane < SSD_HEAD_DIM, lo, hi))
    return jnp.concatenate(cols, axis=1)


def _ssd_chunk(z, window, dt_raw, st, cw_ref, cb_ref, dtb_ref, alog_ref, dsk_ref, gn_ref, interleave):
    n_inner = z.shape[1]
    halo = window.shape[0] - CHUNK

    conv = cb_ref[...] + cw_ref[CONV_K - 1] * window[halo:, :]
    for k in range(CONV_K - 1):
        shifted = pltpu.roll(window, CONV_K - 1 - k, 0)[halo:, :]
        conv = conv + cw_ref[k] * shifted
    xbc = conv * jax.nn.sigmoid(conv)
    xs = xbc[:, :n_inner]
    bm = xbc[:, n_inner:n_inner + LANES]
    cm = xbc[:, n_inner + LANES:]

    row = lax.broadcasted_iota(jnp.int32, (CHUNK, CHUNK), 0)
    col = lax.broadcasted_iota(jnp.int32, (CHUNK, CHUNK), 1)
    causal = row >= col
    lane = col

    dt = _softplus(dt_raw + dtb_ref[...])
    a_c = dt * (-jnp.exp(alog_ref[...]))
    tri = jnp.where(causal, 1.0, 0.0).astype(BF16)
    a_cum = None
    for part in _split_bf16(a_c, 3):
        term = _dot(tri, part)
        a_cum = term if a_cum is None else a_cum + term
    a_cum_t = a_cum.T

    dt_x = _expand_heads(dt, lane)
    acum_x = _expand_heads(a_cum, lane)
    last_x = acum_x[CHUNK - 1:CHUNK, :]
    xc = xs * dt_x
    e_acum = jnp.exp(acum_x)
    xd = (xc * jnp.exp(last_x - acum_x)).astype(BF16)
    chunk_decay = jnp.exp(last_x)
    for job in interleave:
        job()

    bm_b = bm.astype(BF16)
    cbs = []
    for g in range(SSD_GROUPS):
        in_g = (lane >= g * SSD_STATE) & (lane < (g + 1) * SSD_STATE)
        cbs.append(_dot_nt(jnp.where(in_g, cm, 0.0).astype(BF16), bm_b))
    y_cols = []
    for j in range(n_inner // LANES):
        ms = []
        for h in (2 * j, 2 * j + 1):
            seg = a_cum[:, h:h + 1] - a_cum_t[h:h + 1, :]
            decay = jnp.exp(jnp.where(causal, seg, NEG_BIG))
            ms.append((cbs[h * SSD_HEAD_DIM * SSD_GROUPS // n_inner] * decay).astype(BF16))
        xcj = xc[:, j * LANES:(j + 1) * LANES]
        x_lo = jnp.where(lane < SSD_HEAD_DIM, xcj, 0.0).astype(BF16)
        x_hi = jnp.where(lane >= SSD_HEAD_DIM, xcj, 0.0).astype(BF16)
        y_cols.append(_dot(jnp.concatenate(ms, axis=1), jnp.concatenate([x_lo, x_hi], axis=0)))
    y = jnp.concatenate(y_cols, axis=1)

    y = y + _dot(cm.astype(BF16), st.astype(BF16)) * e_acum
    srow = lax.broadcasted_iota(jnp.int32, st.shape, 0)
    scol = lax.broadcasted_iota(jnp.int32, st.shape, 1)
    same_group = (srow // SSD_STATE) == (scol // (n_inner // SSD_GROUPS))
    new_st = st * chunk_decay + jnp.where(same_group, _dot(bm.T.astype(BF16), xd), 0.0)

    y = y + dsk_ref[...] * xs
    y = y * (z * jax.nn.sigmoid(z))
    return _rms(y, gn_ref[...]).astype(BF16), new_st


def _proj_ssd_kernel(h_ref, g_ref, win_ref, cw_ref, cb_ref, dtb_ref, alog_ref, dsk_ref, gn_ref,
                     qkv_ref, y_ref, wzx_ref, wdt_ref, wqkv_ref, zx_a, dt_a, zx_b, dt_b, halo_scr, state,
                     *, tiles_per_batch, n_heads):
    s = pl.program_id(0)
    tm = h_ref.shape[0]
    n_inner = y_ref.shape[1]

    @pl.when(s == 0)
    def _():
        zx_b[...] = jnp.zeros(zx_b.shape, F32)
        dt_b[...] = jnp.zeros(dt_b.shape, F32)
        halo_scr[...] = jnp.zeros(halo_scr.shape, F32)
        state[...] = jnp.zeros(state.shape, F32)
        nzx, nqkv = wzx_ref.shape[0], wqkv_ref.shape[0]
        n_sb = nqkv // 3
        q_scale = 1.0 / math.sqrt(SB_HEAD_DIM)
        for r in range(0, nzx, CHUNK):
            wzx_ref[r:r + CHUNK, :] = win_ref[r:r + CHUNK, :].astype(BF16)
        row = lax.broadcasted_iota(jnp.int32, wdt_ref.shape, 0)
        wdt_ref[...] = jnp.where(row < n_heads, win_ref[nzx:nzx + wdt_ref.shape[0], :], 0.0).astype(BF16)
        for r in range(0, nqkv, CHUNK):
            blk = win_ref[nzx + n_heads + r:nzx + n_heads + r + CHUNK, :]
            wqkv_ref[r:r + CHUNK, :] = (blk * q_scale if r < n_sb else blk).astype(BF16)

    def step(zx_w, dt_w, zx_r, dt_r):
        hn = _rms(h_ref[...], g_ref[...]).astype(BF16)
        n_chunks = tm // CHUNK
        nzx, nqkv = zx_w.shape[1], qkv_ref.shape[1]

        def project(out_ref, w_ref, a):
            b = min(a + MXU_N, w_ref.shape[0])
            out_ref[:, a:b] = _dot_nt(hn, w_ref[a:b, :]).astype(out_ref.dtype)

        jobs = [functools.partial(project, zx_w, wzx_ref, a) for a in range(0, nzx, MXU_N)]
        jobs += [functools.partial(project, dt_w, wdt_ref, 0)]
        jobs += [functools.partial(project, qkv_ref, wqkv_ref, a) for a in range(0, nqkv, MXU_N)]
        share = -(-len(jobs) // n_chunks)

        fresh = lax.rem(s - 1, tiles_per_batch) == 0
        st = jnp.where(fresh, 0.0, state[...])
        halo = jnp.where(fresh, 0.0, halo_scr[...])
        for c in range(n_chunks):
            r0 = c * CHUNK
            if c == 0:
                window = jnp.concatenate([halo, zx_r[0:CHUNK, n_inner:]], axis=0)
            else:
                window = zx_r[r0 - SUBLANES:r0 + CHUNK, n_inner:]
            y, st = _ssd_chunk(zx_r[r0:r0 + CHUNK, :n_inner], window, dt_r[r0:r0 + CHUNK, :], st,
                               cw_ref, cb_ref, dtb_ref, alog_ref, dsk_ref, gn_ref, jobs[c * share:(c + 1) * share])
            y_ref[r0:r0 + CHUNK, :] = y
        state[...] = st
        halo_scr[...] = zx_r[tm - SUBLANES:tm, n_inner:]

    @pl.when(s % 2 == 0)
    def _():
        step(zx_a, dt_a, zx_b, dt_b)

    @pl.when(s % 2 == 1)
    def _():
        step(zx_b, dt_b, zx_a, dt_a)


def _proj_ssd(h, g, w_in, layer, cw, cb, dtb, alog, dsk, gn, batch, n_heads, tm=512):
    t, d = h.shape
    n_inner = gn.shape[1]
    n_conv = cb.shape[1]
    nzx, ndt = n_inner + n_conv, LANES
    in_dim = w_in.shape[1]
    nqkv = in_dim - nzx - n_heads
    n_tiles = t // tm
    last = n_tiles - 1
    resident = lambda shape: pl.BlockSpec(shape, lambda s: (0,) * len(shape), pipeline_mode=pl.Buffered(1))
    return pl.pallas_call(
        functools.partial(_proj_ssd_kernel, tiles_per_batch=n_tiles // batch, n_heads=n_heads),
        grid=(n_tiles + 1,),
        in_specs=[
            pl.BlockSpec((tm, d), lambda s: (jnp.minimum(s, last), 0)),
            resident((1, d)),
            pl.BlockSpec((None, in_dim, d), lambda s: (layer, 0, 0), pipeline_mode=pl.Buffered(1)),
            resident((CONV_K, 1, n_conv)),
            resident((1, n_conv)),
            resident((1, LANES)),
            resident((1, LANES)),
            resident((1, n_inner)),
            resident((1, n_inner)),
        ],
        out_specs=[
            pl.BlockSpec((tm, nqkv), lambda s: (jnp.minimum(s, last), 0)),
            pl.BlockSpec((tm, n_inner), lambda s: (jnp.maximum(s - 1, 0), 0)),
        ],
        out_shape=[
            jax.ShapeDtypeStruct((t, nqkv), BF16),
            jax.ShapeDtypeStruct((t, n_inner), BF16),
        ],
        scratch_shapes=[
            pltpu.VMEM((nzx, d), BF16),
            pltpu.VMEM((ndt, d), BF16),
            pltpu.VMEM((nqkv, d), BF16),
            pltpu.VMEM((tm, nzx), F32),
            pltpu.VMEM((tm, ndt), F32),
            pltpu.VMEM((tm, nzx), F32),
            pltpu.VMEM((tm, ndt), F32),
            pltpu.VMEM((SUBLANES, n_conv), F32),
            pltpu.VMEM((SSD_GROUPS * SSD_STATE, n_inner), F32),
        ],
        compiler_params=_cparams(("arbitrary",)),
        name="proj_ssd",
    )(h, g, w_in, cw, cb, dtb, alog, dsk, gn)


def _sb_kernel(q_ref, k_ref, v_ref, u2_ref, g_ref, o_ref, acc, rsum, z_scr, hl_scr, s_scr, w_scr):
    n_group = q_ref.shape[0]
    n_pairs = q_ref.shape[2] // LANES
    units = [(g, p) for g in range(n_group) for p in range(n_pairs)]
    n_units = len(units)
    n_ahead = SB_STATIC_BLOCKS - 1

    row2 = lax.broadcasted_iota(jnp.int32, (SB_TQ, 2 * SB_TK), 0)
    col2 = lax.broadcasted_iota(jnp.int32, (SB_TQ, 2 * SB_TK), 1)
    below = (col2 & (SB_TK - 1)) < row2
    lane = lax.broadcasted_iota(jnp.int32, (SB_TK, LANES), 1)
    lo_m = jnp.where(lane < SB_HEAD_DIM, 1.0, 0.0).astype(BF16)
    hi_m = jnp.where(lane >= SB_HEAD_DIM, 1.0, 0.0).astype(BF16)

    def block_front(q_rows, j, diagonal, slot):
        off = pl.multiple_of(j * SB_TK, SB_TK)
        base = slot * n_units
        for u, (g, p) in enumerate(units):
            cs = slice(p * LANES, (p + 1) * LANES)
            kb = k_ref[g, pl.ds(off, SB_TK), cs]
            k2 = jnp.concatenate([kb * lo_m, kb * hi_m], axis=0)
            z = _dot_nt(q_ref[g, q_rows, cs], k2)
            z_scr[base + u] = z * LOG2_E
        for u in range(n_units):
            z2 = z_scr[base + u]
            sp2 = jnp.maximum(z2, 0.0) + jnp.log2(1.0 + jnp.exp2(-jnp.abs(z2)))
            if diagonal:
                sp2 = jnp.where(below, sp2, 0.0)
                z_scr[base + u] = jnp.where(below, z2, NEG_BIG)
            hi, lo = _split_bf16(sp2, 2)
            hl_scr[base + u, :, 0:2 * SB_TK] = hi
            hl_scr[base + u, :, 2 * SB_TK:4 * SB_TK] = lo
        for u in range(n_units):
            s_scr[base + u] = _dot(hl_scr[base + u], u2_ref[...])

    def block_back(j, slot):
        off = pl.multiple_of(j * SB_TK, SB_TK)
        base = slot * n_units
        rmax = None
        for u in range(n_units):
            r_prev = jnp.concatenate([rsum[2 * u], rsum[2 * u + 1]], axis=1)
            s = s_scr[base + u] + r_prev
            w_scr[base + u] = jnp.exp2(z_scr[base + u] + s).astype(BF16)
            for e in range(2):
                r_new = jnp.broadcast_to(s[:, e * SB_TK:e * SB_TK + 1], (SB_TQ, LANES))
                rsum[2 * u + e] = r_new
                rmax = r_new if rmax is None else jnp.maximum(rmax, r_new)
        for u, (g, p) in enumerate(units):
            cs = slice(p * LANES, (p + 1) * LANES)
            vb = v_ref[g, pl.ds(off, SB_TK), cs]
            v2 = jnp.concatenate([vb * lo_m, vb * hi_m], axis=0)
            acc[u] += _dot(w_scr[base + u], v2)
        return jnp.max(rmax)

    def walk_rest(q_rows, j_start, rmax_start):
        def cond(carry):
            j, rmax = carry
            return jnp.logical_and(j >= 0, rmax > SB_SKIP_BELOW * LOG2_E)

        def body(carry):
            j, _ = carry
            block_front(q_rows, j, False, 0)
            return j - 1, block_back(j, 0)

        lax.while_loop(cond, body, (j_start, rmax_start))

    def query_block(qb, carry):
        qi = pl.program_id(1) * SB_QB_PER_STEP + qb
        q_rows = pl.ds(pl.multiple_of(qb * SB_TQ, SB_TQ), SB_TQ)
        acc[...] = jnp.zeros(acc.shape, F32)
        rsum[...] = jnp.zeros(rsum.shape, F32)

        @pl.when(qi >= n_ahead)
        def _():
            for b in range(SB_STATIC_BLOCKS):
                block_front(q_rows, qi - b, b == 0, b)
            for b in range(SB_STATIC_BLOCKS):
                rmax = block_back(qi - b, b)
            walk_rest(q_rows, qi - SB_STATIC_BLOCKS, rmax)

        @pl.when(qi < n_ahead)
        def _():
            block_front(q_rows, qi, True, 0)
            walk_rest(q_rows, qi - 1, block_back(qi, 0))

        for g in range(n_group):
            o = jnp.concatenate([acc[g * n_pairs + p] for p in range(n_pairs)], axis=1)
            o_ref[g, q_rows, :] = _rms(o, g_ref[...]).astype(BF16)
        return carry

    lax.fori_loop(0, SB_QB_PER_STEP, query_block, 0)


def _sb_attention(qkv, u2, g, batch):
    t, n3 = qkv.shape
    n = n3 // 3
    s = t // batch
    tq = SB_QB_PER_STEP * SB_TQ
    nq = s // tq
    n_pairs = n // LANES
    group = SB_GROUP if batch % SB_GROUP == 0 else 1
    n_units = group * n_pairs
    qkv3 = qkv.reshape(batch, s, n3)
    out = pl.pallas_call(
        _sb_kernel,
        grid=(batch // group, nq),
        in_specs=[
            pl.BlockSpec((group, tq, n), lambda b, i: (b, i, 0)),
            pl.BlockSpec((group, s, n), lambda b, i: (b, 0, 1)),
            pl.BlockSpec((group, s, n), lambda b, i: (b, 0, 2)),
            pl.BlockSpec((4 * SB_TK, 2 * SB_TK), lambda b, i: (0, 0)),
            pl.BlockSpec((1, n), lambda b, i: (0, 0)),
        ],
        out_specs=pl.BlockSpec((group, tq, n), lambda b, i: (b, i, 0)),
        out_shape=jax.ShapeDtypeStruct((batch, s, n), BF16),
        scratch_shapes=[
            pltpu.VMEM((n_units, SB_TQ, LANES), F32),
            pltpu.VMEM((2 * n_units, SB_TQ, LANES), F32),
            pltpu.VMEM((SB_STATIC_BLOCKS * n_units, SB_TQ, 2 * SB_TK), F32),
            pltpu.VMEM((SB_STATIC_BLOCKS * n_units, SB_TQ, 4 * SB_TK), BF16),
            pltpu.VMEM((SB_STATIC_BLOCKS * n_units, SB_TQ, 2 * SB_TK), F32),
            pltpu.VMEM((SB_STATIC_BLOCKS * n_units, SB_TQ, 2 * SB_TK), BF16),
        ],
        compiler_params=_cparams(("parallel", "arbitrary")),
        name="sb_attn",
    )(qkv3, qkv3, qkv3, u2, g)
    return out.reshape(t, n)


def _memkv_kernel(m_ref, g_ref, wk_ref, wv_ref, k_ref, v_ref):
    mn = _rms(m_ref[...], g_ref[...]).astype(BF16)
    k_ref[...] = _dot(mn, wk_ref[...]).astype(BF16)
    v_ref[...] = _dot(mn, wv_ref[...]).astype(BF16)


def _mem_kv(mem2d, g, wk, wv, layer, batch):
    tmem, d = mem2d.shape
    m = tmem // batch
    n = wk.shape[2]
    const = lambda b: (0, 0)
    row = lambda b: (b, 0)
    return pl.pallas_call(
        _memkv_kernel,
        grid=(batch,),
        in_specs=[
            pl.BlockSpec((m, d), row),
            pl.BlockSpec((1, d), const),
            pl.BlockSpec((None, d, n), lambda b: (layer, 0, 0)),
            pl.BlockSpec((None, d, n), lambda b: (layer, 0, 0)),
        ],
        out_specs=[pl.BlockSpec((m, n), row), pl.BlockSpec((m, n), row)],
        out_shape=[jax.ShapeDtypeStruct((tmem, n), BF16)] * 2,
        compiler_params=_cparams(("parallel",)),
        name="mem_kv",
    )(mem2d, g, wk, wv)


def _tail_kernel(h_ref, ya_ref, yb_ref, wa_ref, wb_ref, gx_ref, wq_ref, k_ref, v_ref, wo_ref, gf_ref, w1_ref, w2_ref,
                 gfin_ref, o_ref, *, ff_chunk, final_norm):
    h = h_ref[...] + _dot(ya_ref[...], wa_ref[...]) + _dot(yb_ref[...], wb_ref[...])

    hn = _rms(h, gx_ref[...]).astype(BF16)
    q = _dot(hn, wq_ref[...]).astype(BF16)
    scale = 1.0 / math.sqrt(XA_HEAD_DIM)
    outs = []
    for hd in range(q.shape[1] // XA_HEAD_DIM):
        cs = slice(hd * XA_HEAD_DIM, (hd + 1) * XA_HEAD_DIM)
        logits = _dot_nt(q[:, cs], k_ref[:, cs]) * scale
        e = jnp.exp(logits - jnp.max(logits, axis=-1, keepdims=True))
        denom = jnp.sum(e, axis=-1, keepdims=True)
        outs.append(_dot(e.astype(BF16), v_ref[:, cs]) / denom)
    h = h + _dot(jnp.concatenate(outs, axis=1).astype(BF16), wo_ref[...])

    hn = _rms(h, gf_ref[...]).astype(BF16)
    out = h
    for c in range(w1_ref.shape[1] // ff_chunk):
        cs = slice(c * ff_chunk, (c + 1) * ff_chunk)
        u = jnp.maximum(_dot(hn, w1_ref[:, cs]), 0.0)
        out = out + _dot((u * u).astype(BF16), w2_ref[cs, :])
    if final_norm:
        out = _rms(out, gfin_ref[...])
    o_ref[...] = out


def _layer_tail(h, ya, yb, w_out, gx, wq, k, v, wo, gf, w1, w2, gfin, layer, batch, final_norm, tm=1024, ff_chunk=1024):
    t, d = h.shape
    na, nb = ya.shape[1], yb.shape[1]
    assert na == nb
    n = wq.shape[2]
    f = w1.shape[2]
    m = k.shape[0] // batch
    tiles_per_batch = t // batch // tm
    row = lambda i: (i, 0)
    kv = lambda i: (i // tiles_per_batch, 0)
    resident = lambda shape: pl.BlockSpec(shape, lambda i: (0, 0), pipeline_mode=pl.Buffered(1))
    weight = lambda shape, r=0: pl.BlockSpec((None,) + shape, lambda i: (layer, r, 0), pipeline_mode=pl.Buffered(1))
    return pl.pallas_call(
        functools.partial(_tail_kernel, ff_chunk=ff_chunk, final_norm=final_norm),
        grid=(t // tm,),
        in_specs=[
            pl.BlockSpec((tm, d), row),
            pl.BlockSpec((tm, na), row),
            pl.BlockSpec((tm, nb), row),
            weight((na, d), 0),
            weight((nb, d), 1),
            resident((1, d)),
            weight((d, n)),
            pl.BlockSpec((m, n), kv),
            pl.BlockSpec((m, n), kv),
            weight((n, d)),
            resident((1, d)),
            weight((d, f)),
            weight((f, d)),
            resident((1, d)),
        ],
        out_specs=pl.BlockSpec((tm, d), row),
        out_shape=jax.ShapeDtypeStruct((t, d), F32),
        compiler_params=_cparams(("parallel",)),
        name="layer_tail",
    )(h, ya, yb, w_out, w_out, gx, wq, k, v, wo, gf, w1, w2, gfin)


def _pad_lanes(v):
    return jnp.pad(v.astype(F32), (0, LANES - v.shape[0])).reshape(1, LANES)


def kernel(x, mem, norm_mix_g, w_in, conv_w, conv_b, dt_bias, a_log, d_skip, ssd_norm_g, sb_norm_g, w_out,
           norm_xa_g, norm_mem_g, w_xq, w_xk, w_xv, w_xo, norm_ff_g, w_ff1, w_ff2, final_g):
    batch, seq, d = x.shape
    depth = w_in.shape[0]
    n_heads = dt_bias.shape[1]
    t = batch * seq
    row = lambda v: v.astype(F32).reshape(1, -1)

    jj = lax.broadcasted_iota(jnp.int32, (2 * SB_TK, 2 * SB_TK), 0)
    ss = lax.broadcasted_iota(jnp.int32, (2 * SB_TK, 2 * SB_TK), 1)
    u2 = -((jj // SB_TK == ss // SB_TK) & (jj >= ss)).astype(BF16)
    u2 = jnp.concatenate([u2, u2], axis=0)

    w_out_b, w_xq_b, w_xk_b, w_xv_b, w_xo_b, w_ff1_b, w_ff2_b = (
        w.astype(BF16) for w in (w_out, w_xq, w_xk, w_xv, w_xo, w_ff1, w_ff2))
    w_in = jnp.swapaxes(w_in.astype(F32), 1, 2)

    h = x.reshape(t, d)
    mem2d = mem.reshape(batch * mem.shape[1], d)
    for l in range(depth):
        qkv, y_ssd = _proj_ssd(h, row(norm_mix_g[l]), w_in, l, conv_w[l].astype(F32)[:, None, :],
                               row(conv_b[l]), _pad_lanes(dt_bias[l]), _pad_lanes(a_log[l]),
                               row(jnp.repeat(d_skip[l], SSD_HEAD_DIM)), row(ssd_norm_g[l]), batch, n_heads)
        y_sb = _sb_attention(qkv, u2, row(sb_norm_g[l]), batch)
        k_mem, v_mem = _mem_kv(mem2d, row(norm_mem_g[l]), w_xk_b, w_xv_b, l, batch)
        h = _layer_tail(h, y_ssd, y_sb, w_out_b, row(norm_xa_g[l]), w_xq_b, k_mem, v_mem, w_xo_b,
                        row(norm_ff_g[l]), w_ff1_b, w_ff2_b, row(final_g), l, batch, final_norm=(l == depth - 1))
    return h.reshape(batch, seq, d)
```

```python
import functools
import math

import jax
import jax.numpy as jnp
from jax import lax
from jax.experimental import pallas as pl
from jax.experimental.pallas import tpu as pltpu

F32 = jnp.float32
BF16 = jnp.bfloat16

EPS = 1e-5
CONV_K = 4
CHUNK = 128
SSD_HEAD_DIM = 64
SSD_GROUPS = 2
SSD_STATE = 64
SB_HEAD_DIM = 64
XA_HEAD_DIM = 128

LANES = 128
SUBLANES = 8
MXU_N = 256
VMEM_LIMIT = 56 * 1024 * 1024

SB_TQ = 128
SB_TK = 128
SB_GROUP = 2
SB_STATIC_BLOCKS = 3
SB_QB_PER_STEP = 4
SB_SKIP_BELOW = -110.0
NEG_BIG = -1e30
LOG2_E = math.log2(math.e)


def _cparams(sem):
    return pltpu.CompilerParams(dimension_semantics=sem, vmem_limit_bytes=VMEM_LIMIT)


def _rms(x, g):
    return x * lax.rsqrt(jnp.mean(x * x, axis=-1, keepdims=True) + EPS) * g


def _softplus(x):
    return jnp.maximum(x, 0.0) + jnp.log(1.0 + jnp.exp(-jnp.abs(x)))


def _dot(a, b):
    return jnp.dot(a, b, preferred_element_type=F32)


def _dot_nt(a, b):
    return lax.dot_general(a, b, (((1,), (1,)), ((), ())), preferred_element_type=F32)


def _split_bf16(x, parts):
    out = []
    r = x
    for _ in range(parts - 1):
        p = r.astype(BF16)
        out.append(p)
        r = r - p.astype(F32)
    out.append(r.astype(BF16))
    return out


def _expand_heads(a, lane):
    cols = []
    for j in range(4):
        lo = jnp.broadcast_to(a[:, 2 * j:2 * j + 1], (a.shape[0], LANES))
        hi = jnp.broadcast_to(a[:, 2 * j + 1:2 * j + 2], (a.shape[0], LANES))
        cols.append(jnp.where(lane < SSD_HEAD_DIM, lo, hi))
    return jnp.concatenate(cols, axis=1)


def _ssd_chunk(z, window, dt_raw, st, cw_ref, cb_ref, dtb_ref, alog_ref, dsk_ref, gn_ref, interleave):
    n_inner = z.shape[1]
    halo = window.shape[0] - CHUNK

    conv = cb_ref[...] + cw_ref[CONV_K - 1] * window[halo:, :]
    for k in range(CONV_K - 1):
        shifted = pltpu.roll(window, CONV_K - 1 - k, 0)[halo:, :]
        conv = conv + cw_ref[k] * shifted
    xbc = conv * jax.nn.sigmoid(conv)
    xs = xbc[:, :n_inner]
    bm = xbc[:, n_inner:n_inner + LANES]
    cm = xbc[:, n_inner + LANES:]

    row = lax.broadcasted_iota(jnp.int32, (CHUNK, CHUNK), 0)
    col = lax.broadcasted_iota(jnp.int32, (CHUNK, CHUNK), 1)
    causal = row >= col
    lane = col

    dt = _softplus(dt_raw + dtb_ref[...])
    a_c = dt * (-jnp.exp(alog_ref[...]))
    tri = jnp.where(causal, 1.0, 0.0).astype(BF16)
    a_cum = None
    for part in _split_bf16(a_c, 3):
        term = _dot(tri, part)
        a_cum = term if a_cum is None else a_cum + term
    a_cum_t = a_cum.T

    dt_x = _expand_heads(dt, lane)
    acum_x = _expand_heads(a_cum, lane)
    last_x = acum_x[CHUNK - 1:CHUNK, :]
    xc = xs * dt_x
    e_acum = jnp.exp(acum_x)
    xd = (xc * jnp.exp(last_x - acum_x)).astype(BF16)
    chunk_decay = jnp.exp(last_x)
    for job in interleave:
        job()

    bm_b = bm.astype(BF16)
    cbs = []
    for g in range(SSD_GROUPS):
        in_g = (lane >= g * SSD_STATE) & (lane < (g + 1) * SSD_STATE)
        cbs.append(_dot_nt(jnp.where(in_g, cm, 0.0).astype(BF16), bm_b))
    y_cols = []
    for j in range(n_inner // LANES):
        ms = []
        for h in (2 * j, 2 * j + 1):
            seg = a_cum[:, h:h + 1] - a_cum_t[h:h + 1, :]
            decay = jnp.exp(jnp.where(causal, seg, NEG_BIG))
            ms.append((cbs[h * SSD_HEAD_DIM * SSD_GROUPS // n_inner] * decay).astype(BF16))
        xcj = xc[:, j * LANES:(j + 1) * LANES]
        x_lo = jnp.where(lane < SSD_HEAD_DIM, xcj, 0.0).astype(BF16)
        x_hi = jnp.where(lane >= SSD_HEAD_DIM, xcj, 0.0).astype(BF16)
        y_cols.append(_dot(jnp.concatenate(ms, axis=1), jnp.concatenate([x_lo, x_hi], axis=0)))
    y = jnp.concatenate(y_cols, axis=1)

    y = y + _dot(cm.astype(BF16), st.astype(BF16)) * e_acum
    srow = lax.broadcasted_iota(jnp.int32, st.shape, 0)
    scol = lax.broadcasted_iota(jnp.int32, st.shape, 1)
    same_group = (srow // SSD_STATE) == (scol // (n_inner // SSD_GROUPS))
    new_st = st * chunk_decay + jnp.where(same_group, _dot(bm.T.astype(BF16), xd), 0.0)

    y = y + dsk_ref[...] * xs
    y = y * (z * jax.nn.sigmoid(z))
    return _rms(y, gn_ref[...]).astype(BF16), new_st


def _proj_ssd_kernel(h_ref, g_ref, win_ref, cw_ref, cb_ref, dtb_ref, alog_ref, dsk_ref, gn_ref,
                     qkv_ref, y_ref, wzx_ref, wdt_ref, wqkv_ref, zx_a, dt_a, zx_b, dt_b, halo_scr, state,
                     *, tiles_per_batch, n_heads):
    s = pl.program_id(0)
    tm = h_ref.shape[0]
    n_inner = y_ref.shape[1]

    @pl.when(s == 0)
    def _():
        zx_b[...] = jnp.zeros(zx_b.shape, F32)
        dt_b[...] = jnp.zeros(dt_b.shape, F32)
        halo_scr[...] = jnp.zeros(halo_scr.shape, F32)
        state[...] = jnp.zeros(state.shape, F32)
        nzx, nqkv = wzx_ref.shape[0], wqkv_ref.shape[0]
        n_sb = nqkv // 3
        q_scale = 1.0 / math.sqrt(SB_HEAD_DIM)
        for r in range(0, nzx, CHUNK):
            wzx_ref[r:r + CHUNK, :] = win_ref[r:r + CHUNK, :].astype(BF16)
        row = lax.broadcasted_iota(jnp.int32, wdt_ref.shape, 0)
        wdt_ref[...] = jnp.where(row < n_heads, win_ref[nzx:nzx + wdt_ref.shape[0], :], 0.0).astype(BF16)
        for r in range(0, nqkv, CHUNK):
            blk = win_ref[nzx + n_heads + r:nzx + n_heads + r + CHUNK, :]
            wqkv_ref[r:r + CHUNK, :] = (blk * q_scale if r < n_sb else blk).astype(BF16)

    def step(zx_w, dt_w, zx_r, dt_r):
        hn = _rms(h_ref[...], g_ref[...]).astype(BF16)
        n_chunks = tm // CHUNK
        nzx, nqkv = zx_w.shape[1], qkv_ref.shape[1]

        def project(out_ref, w_ref, a):
            b = min(a + MXU_N, w_ref.shape[0])
            out_ref[:, a:b] = _dot_nt(hn, w_ref[a:b, :]).astype(out_ref.dtype)

        jobs = [functools.partial(project, zx_w, wzx_ref, a) for a in range(0, nzx, MXU_N)]
        jobs += [functools.partial(project, dt_w, wdt_ref, 0)]
        jobs += [functools.partial(project, qkv_ref, wqkv_ref, a) for a in range(0, nqkv, MXU_N)]
        share = -(-len(jobs) // n_chunks)

        fresh = lax.rem(s - 1, tiles_per_batch) == 0
        st = jnp.where(fresh, 0.0, state[...])
        halo = jnp.where(fresh, 0.0, halo_scr[...])
        for c in range(n_chunks):
            r0 = c * CHUNK
            if c == 0:
                window = jnp.concatenate([halo, zx_r[0:CHUNK, n_inner:]], axis=0)
            else:
                window = zx_r[r0 - SUBLANES:r0 + CHUNK, n_inner:]
            y, st = _ssd_chunk(zx_r[r0:r0 + CHUNK, :n_inner], window, dt_r[r0:r0 + CHUNK, :], st,
                               cw_ref, cb_ref, dtb_ref, alog_ref, dsk_ref, gn_ref, jobs[c * share:(c + 1) * share])
            y_ref[r0:r0 + CHUNK, :] = y
        state[...] = st
        halo_scr[...] = zx_r[tm - SUBLANES:tm, n_inner:]

    @pl.when(s % 2 == 0)
    def _():
        step(zx_a, dt_a, zx_b, dt_b)

    @pl.when(s % 2 == 1)
    def _():
        step(zx_b, dt_b, zx_a, dt_a)


def _proj_ssd(h, g, w_in, layer, cw, cb, dtb, alog, dsk, gn, batch, n_heads, tm=512):
    t, d = h.shape
    n_inner = gn.shape[1]
    n_conv = cb.shape[1]
    nzx, ndt = n_inner + n_conv, LANES
    in_dim = w_in.shape[1]
    nqkv = in_dim - nzx - n_heads
    n_tiles = t // tm
    last = n_tiles - 1
    resident = lambda shape: pl.BlockSpec(shape, lambda s: (0,) * len(shape), pipeline_mode=pl.Buffered(1))
    return pl.pallas_call(
        functools.partial(_proj_ssd_kernel, tiles_per_batch=n_tiles // batch, n_heads=n_heads),
        grid=(n_tiles + 1,),
        in_specs=[
            pl.BlockSpec((tm, d), lambda s: (jnp.minimum(s, last), 0)),
            resident((1, d)),
            pl.BlockSpec((None, in_dim, d), lambda s: (layer, 0, 0), pipeline_mode=pl.Buffered(1)),
            resident((CONV_K, 1, n_conv)),
            resident((1, n_conv)),
            resident((1, LANES)),
            resident((1, LANES)),
            resident((1, n_inner)),
            resident((1, n_inner)),
        ],
        out_specs=[
            pl.BlockSpec((tm, nqkv), lambda s: (jnp.minimum(s, last), 0)),
            pl.BlockSpec((tm, n_inner), lambda s: (jnp.maximum(s - 1, 0), 0)),
        ],
        out_shape=[
            jax.ShapeDtypeStruct((t, nqkv), BF16),
            jax.ShapeDtypeStruct((t, n_inner), BF16),
        ],
        scratch_shapes=[
            pltpu.VMEM((nzx, d), BF16),
            pltpu.VMEM((ndt, d), BF16),
            pltpu.VMEM((nqkv, d), BF16),
            pltpu.VMEM((tm, nzx), F32),
            pltpu.VMEM((tm, ndt), F32),
            pltpu.VMEM((tm, nzx), F32),
            pltpu.VMEM((tm, ndt), F32),
            pltpu.VMEM((SUBLANES, n_conv), F32),
            pltpu.VMEM((SSD_GROUPS * SSD_STATE, n_inner), F32),
        ],
        compiler_params=_cparams(("arbitrary",)),
        name="proj_ssd",
    )(h, g, w_in, cw, cb, dtb, alog, dsk, gn)


def _sb_kernel(q_ref, k_ref, v_ref, u2_ref, g_ref, o_ref, acc, rsum, z_scr, hl_scr, s_scr, w_scr):
    n_group = q_ref.shape[0]
    n_pairs = q_ref.shape[2] // LANES
    units = [(g, p) for g in range(n_group) for p in range(n_pairs)]
    n_units = len(units)
    n_ahead = SB_STATIC_BLOCKS - 1

    row2 = lax.broadcasted_iota(jnp.int32, (SB_TQ, 2 * SB_TK), 0)
    col2 = lax.broadcasted_iota(jnp.int32, (SB_TQ, 2 * SB_TK), 1)
    below = (col2 & (SB_TK - 1)) < row2
    lane = lax.broadcasted_iota(jnp.int32, (SB_TK, LANES), 1)
    lo_m = jnp.where(lane < SB_HEAD_DIM, 1.0, 0.0).astype(BF16)
    hi_m = jnp.where(lane >= SB_HEAD_DIM, 1.0, 0.0).astype(BF16)

    def block_front(q_rows, j, diagonal, slot):
        off = pl.multiple_of(j * SB_TK, SB_TK)
        base = slot * n_units
        for u, (g, p) in enumerate(units):
            cs = slice(p * LANES, (p + 1) * LANES)
            kb = k_ref[g, pl.ds(off, SB_TK), cs]
            k2 = jnp.concatenate([kb * lo_m, kb * hi_m], axis=0)
            z = _dot_nt(q_ref[g, q_rows, cs], k2)
            z_scr[base + u] = z * LOG2_E
        for u in range(n_units):
            z2 = z_scr[base + u]
            sp2 = jnp.maximum(z2, 0.0) + jnp.log2(1.0 + jnp.exp2(-jnp.abs(z2)))
            if diagonal:
                sp2 = jnp.where(below, sp2, 0.0)
                z_scr[base + u] = jnp.where(below, z2, NEG_BIG)
            hi, lo = _split_bf16(sp2, 2)
            hl_scr[base + u, :, 0:2 * SB_TK] = hi
            hl_scr[base + u, :, 2 * SB_TK:4 * SB_TK] = lo
        for u in range(n_units):
            s_scr[base + u] = _dot(hl_scr[base + u], u2_ref[...])

    def block_back(j, slot):
        off = pl.multiple_of(j * SB_TK, SB_TK)
        base = slot * n_units
        rmax = None
        for u in range(n_units):
            r_prev = jnp.concatenate([rsum[2 * u], rsum[2 * u + 1]], axis=1)
            s = s_scr[base + u] + r_prev
            w_scr[base + u] = jnp.exp2(z_scr[base + u] + s).astype(BF16)
            for e in range(2):
                r_new = jnp.broadcast_to(s[:, e * SB_TK:e * SB_TK + 1], (SB_TQ, LANES))
                rsum[2 * u + e] = r_new
                rmax = r_new if rmax is None else jnp.maximum(rmax, r_new)
        for u, (g, p) in enumerate(units):
            cs = slice(p * LANES, (p + 1) * LANES)
            vb = v_ref[g, pl.ds(off, SB_TK), cs]
            v2 = jnp.concatenate([vb * lo_m, vb * hi_m], axis=0)
            acc[u] += _dot(w_scr[base + u], v2)
        return jnp.max(rmax)

    def walk_rest(q_rows, j_start, rmax_start):
        def cond(carry):
            j, rmax = carry
            return jnp.logical_and(j >= 0, rmax > SB_SKIP_BELOW * LOG2_E)

        def body(carry):
            j, _ = carry
            block_front(q_rows, j, False, 0)
            return j - 1, block_back(j, 0)

        lax.while_loop(cond, body, (j_start, rmax_start))

    def query_block(qb, carry):
        qi = pl.program_id(1) * SB_QB_PER_STEP + qb
        q_rows = pl.ds(pl.multiple_of(qb * SB_TQ, SB_TQ), SB_TQ)
        acc[...] = jnp.zeros(acc.shape, F32)
        rsum[...] = jnp.zeros(rsum.shape, F32)

        @pl.when(qi >= n_ahead)
        def _():
            for b in range(SB_STATIC_BLOCKS):
                block_front(q_rows, qi - b, b == 0, b)
            for b in range(SB_STATIC_BLOCKS):
                rmax = block_back(qi - b, b)
            walk_rest(q_rows, qi - SB_STATIC_BLOCKS, rmax)

        @pl.when(qi < n_ahead)
        def _():
            block_front(q_rows, qi, True, 0)
            walk_rest(q_rows, qi - 1, block_back(qi, 0))

        for g in range(n_group):
            o = jnp.concatenate([acc[g * n_pairs + p] for p in range(n_pairs)], axis=1)
            o_ref[g, q_rows, :] = _rms(o, g_ref[...]).astype(BF16)
        return carry

    lax.fori_loop(0, SB_QB_PER_STEP, query_block, 0)


def _sb_attention(qkv, u2, g, batch):
    t, n3 = qkv.shape
    n = n3 // 3
    s = t // batch
    tq = SB_QB_PER_STEP * SB_TQ
    nq = s // tq
    n_pairs = n // LANES
    group = SB_GROUP if batch % SB_GROUP == 0 else 1
    n_units = group * n_pairs
    qkv3 = qkv.reshape(batch, s, n3)
    out = pl.pallas_call(
        _sb_kernel,
        grid=(batch // group, nq),
        in_specs=[
            pl.BlockSpec((group, tq, n), lambda b, i: (b, i, 0)),
            pl.BlockSpec((group, s, n), lambda b, i: (b, 0, 1)),
            pl.BlockSpec((group, s, n), lambda b, i: (b, 0, 2)),
            pl.BlockSpec((4 * SB_TK, 2 * SB_TK), lambda b, i: (0, 0)),
            pl.BlockSpec((1, n), lambda b, i: (0, 0)),
        ],
        out_specs=pl.BlockSpec((group, tq, n), lambda b, i: (b, i, 0)),
        out_shape=jax.ShapeDtypeStruct((batch, s, n), BF16),
        scratch_shapes=[
            pltpu.VMEM((n_units, SB_TQ, LANES), F32),
            pltpu.VMEM((2 * n_units, SB_TQ, LANES), F32),
            pltpu.VMEM((SB_STATIC_BLOCKS * n_units, SB_TQ, 2 * SB_TK), F32),
            pltpu.VMEM((SB_STATIC_BLOCKS * n_units, SB_TQ, 4 * SB_TK), BF16),
            pltpu.VMEM((SB_STATIC_BLOCKS * n_units, SB_TQ, 2 * SB_TK), F32),
            pltpu.VMEM((SB_STATIC_BLOCKS * n_units, SB_TQ, 2 * SB_TK), BF16),
        ],
        compiler_params=_cparams(("parallel", "arbitrary")),
        name="sb_attn",
    )(qkv3, qkv3, qkv3, u2, g)
    return out.reshape(t, n)


def _memkv_kernel(m_ref, g_ref, wk_ref, wv_ref, k_ref, v_ref):
    mn = _rms(m_ref[...], g_ref[...]).astype(BF16)
    k_ref[...] = _dot(mn, wk_ref[...]).astype(BF16)
    v_ref[...] = _dot(mn, wv_ref[...]).astype(BF16)


def _mem_kv(mem2d, g, wk, wv, layer, batch):
    tmem, d = mem2d.shape
    m = tmem // batch
    n = wk.shape[2]
    const = lambda b: (0, 0)
    row = lambda b: (b, 0)
    return pl.pallas_call(
        _memkv_kernel,
        grid=(batch,),
        in_specs=[
            pl.BlockSpec((m, d), row),
            pl.BlockSpec((1, d), const),
            pl.BlockSpec((None, d, n), lambda b: (layer, 0, 0)),
            pl.BlockSpec((None, d, n), lambda b: (layer, 0, 0)),
        ],
        out_specs=[pl.BlockSpec((m, n), row), pl.BlockSpec((m, n), row)],
        out_shape=[jax.ShapeDtypeStruct((tmem, n), BF16)] * 2,
        compiler_params=_cparams(("parallel",)),
        name="mem_kv",
    )(mem2d, g, wk, wv)


def _tail_kernel(h_ref, ya_ref, yb_ref, wa_ref, wb_ref, gx_ref, wq_ref, k_ref, v_ref, wo_ref, gf_ref, w1_ref, w2_ref,
                 gfin_ref, o_ref, *, ff_chunk, final_norm):
    h = h_ref[...] + _dot(ya_ref[...], wa_ref[...]) + _dot(yb_ref[...], wb_ref[...])

    hn = _rms(h, gx_ref[...]).astype(BF16)
    q = _dot(hn, wq_ref[...]).astype(BF16)
    scale = 1.0 / math.sqrt(XA_HEAD_DIM)
    outs = []
    for hd in range(q.shape[1] // XA_HEAD_DIM):
        cs = slice(hd * XA_HEAD_DIM, (hd + 1) * XA_HEAD_DIM)
        logits = _dot_nt(q[:, cs], k_ref[:, cs]) * scale
        e = jnp.exp(logits - jnp.max(logits, axis=-1, keepdims=True))
        denom = jnp.sum(e, axis=-1, keepdims=True)
        outs.append(_dot(e.astype(BF16), v_ref[:, cs]) / denom)
    h = h + _dot(jnp.concatenate(outs, axis=1).astype(BF16), wo_ref[...])

    hn = _rms(h, gf_ref[...]).astype(BF16)
    half = h.shape[0] // 2
    for r in range(2):
        rs = slice(r * half, (r + 1) * half)
        out = h[rs]
        for c in range(w1_ref.shape[1] // ff_chunk):
            cs = slice(c * ff_chunk, (c + 1) * ff_chunk)
            u = jnp.maximum(_dot(hn[rs], w1_ref[:, cs]), 0.0)
            out = out + _dot((u * u).astype(BF16), w2_ref[cs, :])
        if final_norm:
            out = _rms(out, gfin_ref[...])
        o_ref[rs, :] = out


def _layer_tail(h, ya, yb, w_out, gx, wq, k, v, wo, gf, w1, w2, gfin, layer, batch, final_norm, tm=1024, ff_chunk=1024):
    t, d = h.shape
    na, nb = ya.shape[1], yb.shape[1]
    assert na == nb
    n = wq.shape[2]
    f = w1.shape[2]
    m = k.shape[0] // batch
    tiles_per_batch = t // batch // tm
    row = lambda i: (i, 0)
    kv = lambda i: (i // tiles_per_batch, 0)
    resident = lambda shape: pl.BlockSpec(shape, lambda i: (0, 0), pipeline_mode=pl.Buffered(1))
    weight = lambda shape, r=0: pl.BlockSpec((None,) + shape, lambda i: (layer, r, 0), pipeline_mode=pl.Buffered(1))
    return pl.pallas_call(
        functools.partial(_tail_kernel, ff_chunk=ff_chunk, final_norm=final_norm),
        grid=(t // tm,),
        in_specs=[
            pl.BlockSpec((tm, d), row),
            pl.BlockSpec((tm, na), row),
            pl.BlockSpec((tm, nb), row),
            weight((na, d), 0),
            weight((nb, d), 1),
            resident((1, d)),
            weight((d, n)),
            pl.BlockSpec((m, n), kv),
            pl.BlockSpec((m, n), kv),
            weight((n, d)),
            resident((1, d)),
            weight((d, f)),
            weight((f, d)),
            resident((1, d)),
        ],
        out_specs=pl.BlockSpec((tm, d), row),
        out_shape=jax.ShapeDtypeStruct((t, d), F32),
        compiler_params=_cparams(("parallel",)),
        name="layer_tail",
    )(h, ya, yb, w_out, w_out, gx, wq, k, v, wo, gf, w1, w2, gfin)


def _pad_lanes(v):
    return jnp.pad(v.astype(F32), (0, LANES - v.shape[0])).reshape(1, LANES)


def kernel(x, mem, norm_mix_g, w_in, conv_w, conv_b, dt_bias, a_log, d_skip, ssd_norm_g, sb_norm_g, w_out,
           norm_xa_g, norm_mem_g, w_xq, w_xk, w_xv, w_xo, norm_ff_g, w_ff1, w_ff2, final_g):
    batch, seq, d = x.shape
    depth = w_in.shape[0]
    n_heads = dt_bias.shape[1]
    t = batch * seq
    row = lambda v: v.astype(F32).reshape(1, -1)

    jj = lax.broadcasted_iota(jnp.int32, (2 * SB_TK, 2 * SB_TK), 0)
    ss = lax.broadcasted_iota(jnp.int32, (2 * SB_TK, 2 * SB_TK), 1)
    u2 = -((jj // SB_TK == ss // SB_TK) & (jj >= ss)).astype(BF16)
    u2 = jnp.concatenate([u2, u2], axis=0)

    w_out_b, w_xq_b, w_xk_b, w_xv_b, w_xo_b, w_ff1_b, w_ff2_b = (
        w.astype(BF16) for w in (w_out, w_xq, w_xk, w_xv, w_xo, w_ff1, w_ff2))
    w_in = jnp.swapaxes(w_in.astype(F32), 1, 2)

    h = x.reshape(t, d)
    mem2d = mem.reshape(batch * mem.shape[1], d)
    for l in range(depth):
        qkv, y_ssd = _proj_ssd(h, row(norm_mix_g[l]), w_in, l, conv_w[l].astype(F32)[:, None, :],
                               row(conv_b[l]), _pad_lanes(dt_bias[l]), _pad_lanes(a_log[l]),
                               row(jnp.repeat(d_skip[l], SSD_HEAD_DIM)), row(ssd_norm_g[l]), batch, n_heads)
        y_sb = _sb_attention(qkv, u2, row(sb_norm_g[l]), batch)
        k_mem, v_mem = _mem_kv(mem2d, row(norm_mem_g[l]), w_xk_b, w_xv_b, l, batch)
        h = _layer_tail(h, y_ssd, y_sb, w_out_b, row(norm_xa_g[l]), w_xq_b, k_mem, v_mem, w_xo_b,
                        row(norm_ff_g[l]), w_ff1_b, w_ff2_b, row(final_g), l, batch, final_norm=(l == depth - 1))
    return h.reshape(batch, seq, d)
```
